```python
import math
import jax, jax.numpy as jnp
from jax import lax
import numpy as np

D_MODEL = 1024
BATCH = 16
SEQ = 2048
DEPTH = 2
DEC_BATCH = 16
DEC_SEQ = 64
PAST_LEN = 4096

CHUNK = 64
N_A = DEPTH // 2
N_B = DEPTH - N_A
EPS = 1e-6
D_INNER = 2 * D_MODEL
SSM_HEAD_DIM = 64
SSM_HEADS = D_INNER // SSM_HEAD_DIM
SSM_GROUPS = 4
HEADS_PER_GROUP = SSM_HEADS // SSM_GROUPS
SSM_STATE = 128
CONV_W = 4
CONV_DIM = D_INNER + 2 * SSM_GROUPS * SSM_STATE
IN_PROJ = D_INNER + CONV_DIM + SSM_HEADS
ATT_HEAD_DIM = 64
ATT_HEADS = D_MODEL // ATT_HEAD_DIM
ATT_DIM = ATT_HEADS * ATT_HEAD_DIM
BAND_CHUNKS = 8
BAND_PAST = BAND_CHUNKS * CHUNK
REL_CLIP = 256
FFN_HIDDEN = ((8 * D_MODEL + 3 * 256 - 1) // (3 * 256)) * 256

kernel_name = 'hybrid_ssd_chunkband_stream_step'


def rmsnorm(x, w):
    xf = x.astype(jnp.float32)
    y = xf * lax.rsqrt(jnp.mean(xf * xf, axis=-1, keepdims=True) + EPS)
    return (y * w.astype(jnp.float32)).astype(x.dtype)


def swiglu(u, w_in, w_out):
    g, up = jnp.split(u @ w_in, 2, axis=-1)
    return (jax.nn.silu(g) * up) @ w_out


def ssd_scan(x, dt, A, Bm, Cm, h0):
    b, L = x.shape[:2]
    Q = CHUNK if L % CHUNK == 0 else L
    nc = L // Q
    G, J, P, N = SSM_GROUPS, HEADS_PER_GROUP, SSM_HEAD_DIM, SSM_STATE
    xd = (x * dt[..., None]).reshape(b, nc, Q, G, J, P)
    Bc = Bm.reshape(b, nc, Q, G, N)
    Cc = Cm.reshape(b, nc, Q, G, N)
    dA = jnp.moveaxis((dt * A).reshape(b, nc, Q, G, J), 2, -1)
    Acs = jnp.cumsum(dA, axis=-1)
    causal = jnp.tril(jnp.ones((Q, Q), dtype=bool))
    Lm = jnp.exp(jnp.where(causal, Acs[..., :, None] - Acs[..., None, :], -jnp.inf))
    CB = jnp.einsum('bcqgn,bcsgn->bcgqs', Cc, Bc)
    y_diag = jnp.einsum('bcgjqs,bcsgjp->bcqgjp', CB[:, :, :, None] * Lm, xd)
    decay_to_end = jnp.moveaxis(jnp.exp(Acs[..., -1:] - Acs), -1, 2)
    chunk_states = jnp.einsum('bcsgn,bcsgjp->cbgjpn', Bc, xd * decay_to_end[..., None])
    chunk_decay = jnp.moveaxis(jnp.exp(Acs[..., -1]), 1, 0)

    def step(h, inp):
        s, d = inp
        return h * d[..., None, None] + s, h

    h_final, h_prev = lax.scan(step, h0.reshape(b, G, J, P, N), (chunk_states, chunk_decay))
    y_off = jnp.einsum('bcqgn,cbgjpn->bcqgjp', Cc, h_prev) * jnp.moveaxis(jnp.exp(Acs), -1, 2)[..., None]
    y = (y_diag + y_off).reshape(b, L, SSM_HEADS, P)
    return y, h_final.reshape(b, SSM_HEADS, P, N)


def ssd_mixer(u, h0, conv_prev, w_in, conv_w, conv_b, dt_bias, A_log, D_skip, gnorm_w, w_out):
    b, L, _ = u.shape
    zxbcdt = u @ w_in
    z = zxbcdt[..., :D_INNER]
    xbc = zxbcdt[..., D_INNER:D_INNER + CONV_DIM]
    dt_raw = zxbcdt[..., D_INNER + CONV_DIM:]
    padded = jnp.concatenate([conv_prev.astype(xbc.dtype), xbc], axis=1)
    conv = conv_b
    for k in range(CONV_W):
        conv = conv + padded[:, k:k + L] * conv_w[k]
    new_conv = padded[:, L:]
    xbc = jax.nn.silu(conv).astype(jnp.float32)
    xs = xbc[..., :D_INNER].reshape(b, L, SSM_HEADS, SSM_HEAD_DIM)
    Bm = xbc[..., D_INNER:D_INNER + SSM_GROUPS * SSM_STATE].reshape(b, L, SSM_GROUPS, SSM_STATE)
    Cm = xbc[..., D_INNER + SSM_GROUPS * SSM_STATE:].reshape(b, L, SSM_GROUPS, SSM_STATE)
    dt = jax.nn.softplus(dt_raw.astype(jnp.float32) + dt_bias.astype(jnp.float32))
    A = -jnp.exp(A_log.astype(jnp.float32))
    y, h_final = ssd_scan(xs, dt, A, Bm, Cm, h0.astype(jnp.float32))
    y = y + xs * D_skip.astype(jnp.float32)[:, None]
    y = y.reshape(b, L, D_INNER) * jax.nn.silu(z.astype(jnp.float32))
    yg = y.reshape(b, L, SSM_GROUPS, D_INNER // SSM_GROUPS)
    yg = yg * lax.rsqrt(jnp.mean(yg * yg, axis=-1, keepdims=True) + EPS)
    y = (yg.reshape(b, L, D_INNER) * gnorm_w.astype(jnp.float32)).astype(u.dtype)
    return y @ w_out, h_final.astype(u.dtype), new_conv


def shared_kv(x, kv_norm_w, w_kv):
    b, L, _ = x.shape
    kv = (rmsnorm(x, kv_norm_w) @ w_kv).reshape(b, L, 2, ATT_HEADS, ATT_HEAD_DIM)
    return kv[:, :, 0], kv[:, :, 1]


def band_attend(q, k, v, qpos, kpos, rel_bias):
    dist = jnp.clip(qpos[:, None] - kpos[None, :], -REL_CLIP, REL_CLIP) + REL_CLIP
    bias = rel_bias[:, dist].astype(jnp.float32)
    qc = qpos // CHUNK
    kc = kpos // CHUNK
    valid = (kpos[None, :] >= 0) & (kc[None, :] <= qc[:, None]) & (kc[None, :] >= qc[:, None] - BAND_CHUNKS)
    s = jnp.einsum('bqhd,bkhd->bhqk', q, k).astype(jnp.float32) * (ATT_HEAD_DIM ** -0.5) + bias
    s = jnp.where(valid, s, -jnp.inf)
    p = jax.nn.softmax(s, axis=-1).astype(v.dtype)
    return jnp.einsum('bhqk,bkhd->bqhd', p, v)


def prompt_band_attention(q, k, v, rel_bias):
    b, L = q.shape[:2]
    nc = L // CHUNK
    band = BAND_PAST + CHUNK
    pad = jnp.zeros((b, BAND_PAST, ATT_HEADS, ATT_HEAD_DIM), k.dtype)
    kp = jnp.concatenate([pad, k], axis=1)
    vp = jnp.concatenate([pad, v], axis=1)

    def one_chunk(c):
        start = c * CHUNK
        qc = lax.dynamic_slice_in_dim(q, start, CHUNK, axis=1)
        kc = lax.dynamic_slice_in_dim(kp, start, band, axis=1)
        vc = lax.dynamic_slice_in_dim(vp, start, band, axis=1)
        qpos = start + jnp.arange(CHUNK)
        kpos = start - BAND_PAST + jnp.arange(band)
        return band_attend(qc, kc, vc, qpos, kpos, rel_bias)

    out = lax.map(one_chunk, jnp.arange(nc))
    return jnp.moveaxis(out, 0, 1).reshape(b, L, ATT_DIM)


def sample_band_attention(q, k_new, v_new, k_cache, v_cache, rel_bias):
    b, L = q.shape[:2]
    R = k_cache.shape[1]
    k = jnp.concatenate([k_cache.astype(k_new.dtype), k_new], axis=1)
    v = jnp.concatenate([v_cache.astype(v_new.dtype), v_new], axis=1)
    qpos = PAST_LEN + jnp.arange(L)
    kpos = jnp.concatenate([PAST_LEN - R + jnp.arange(R), PAST_LEN + jnp.arange(L)])
    return band_attend(q, k, v, qpos, kpos, rel_bias).reshape(b, L, ATT_DIM)


def setup_inputs(seed: int = 0) -> dict:
    key = jax.random.key(seed)
    ks = jax.random.split(key, 24)
    f32 = jnp.float32

    def nrm(k, shape, scale):
        return scale * jax.random.normal(k, shape, f32)

    cache_rows = min(BAND_PAST, PAST_LEN)
    dt0 = jnp.exp(jax.random.uniform(ks[10], (N_A, SSM_HEADS), f32, math.log(1e-3), math.log(1e-1)))
    dt_bias = dt0 + jnp.log(-jnp.expm1(-dt0))
    A_log = jnp.log(jax.random.uniform(ks[11], (N_A, SSM_HEADS), f32, 1.0, 16.0))
    return {
        'x_prompt': nrm(ks[0], (BATCH, SEQ, D_MODEL), 1.0),
        'x_sample': nrm(ks[1], (DEC_BATCH, DEC_SEQ, D_MODEL), 1.0),
        'state_ssm': nrm(ks[2], (N_A, DEC_BATCH, SSM_HEADS, SSM_HEAD_DIM, SSM_STATE), 0.5),
        'state_conv': nrm(ks[3], (N_A, DEC_BATCH, CONV_W - 1, CONV_DIM), 1.0),
        'cache_k': nrm(ks[4], (DEC_BATCH, cache_rows, ATT_HEADS, ATT_HEAD_DIM), 1.0),
        'cache_v': nrm(ks[5], (DEC_BATCH, cache_rows, ATT_HEADS, ATT_HEAD_DIM), 1.0),
        'norm_w': 1.0 + nrm(ks[6], (DEPTH, 4, D_MODEL), 0.05),
        'ssm_w_in': nrm(ks[7], (N_A, D_MODEL, IN_PROJ), D_MODEL ** -0.5),
        'ssm_conv_w': nrm(ks[8], (N_A, CONV_W, CONV_DIM), CONV_W ** -0.5),
        'ssm_conv_b': nrm(ks[9], (N_A, CONV_DIM), 0.02),
        'ssm_dt_bias': dt_bias,
        'ssm_A_log': A_log,
        'ssm_D': 1.0 + nrm(ks[12], (N_A, SSM_HEADS), 0.1),
        'ssm_norm_w': 1.0 + nrm(ks[13], (N_A, D_INNER), 0.05),
        'ssm_w_out': nrm(ks[14], (N_A, D_INNER, D_MODEL), D_INNER ** -0.5),
        'kv_norm_w': 1.0 + nrm(ks[15], (D_MODEL,), 0.05),
        'w_kv': nrm(ks[16], (D_MODEL, 2 * ATT_DIM), D_MODEL ** -0.5),
        'attn_w_q': nrm(ks[17], (N_B, D_MODEL, ATT_DIM), D_MODEL ** -0.5),
        'attn_rel_bias': nrm(ks[18], (N_B, ATT_HEADS, 2 * REL_CLIP + 1), 0.3),
        'attn_w_o': nrm(ks[19], (N_B, ATT_DIM, D_MODEL), ATT_DIM ** -0.5),
        'ffn_w_in': nrm(ks[20], (DEPTH, D_MODEL, 2 * FFN_HIDDEN), D_MODEL ** -0.5),
        'ffn_w_out': nrm(ks[21], (DEPTH, FFN_HIDDEN, D_MODEL), FFN_HIDDEN ** -0.5),
    }


def reference(x_prompt, x_sample, state_ssm, state_conv, cache_k, cache_v, norm_w,
              ssm_w_in, ssm_conv_w, ssm_conv_b, ssm_dt_bias, ssm_A_log, ssm_D, ssm_norm_w, ssm_w_out,
              kv_norm_w, w_kv, attn_w_q, attn_rel_bias, attn_w_o, ffn_w_in, ffn_w_out):
    xp, xs = x_prompt, x_sample
    bp, bs = xp.shape[0], xs.shape[0]
    ssm_p, conv_p, ssm_s, conv_s = [], [], [], []
    kp = vp = ks = vs = None
    for layer in range(DEPTH):
        nw = norm_w[layer]
        if layer < N_A:
            a = layer
            params = (ssm_w_in[a], ssm_conv_w[a], ssm_conv_b[a], ssm_dt_bias[a], ssm_A_log[a],
                      ssm_D[a], ssm_norm_w[a], ssm_w_out[a])
            h0 = jnp.zeros((bp, SSM_HEADS, SSM_HEAD_DIM, SSM_STATE), jnp.float32)
            c0 = jnp.zeros((bp, CONV_W - 1, CONV_DIM), xp.dtype)
            mp, hp_new, cp_new = ssd_mixer(rmsnorm(xp, nw[0]), h0, c0, *params)
            ms, hs_new, cs_new = ssd_mixer(rmsnorm(xs, nw[0]), state_ssm[a], state_conv[a], *params)
            ssm_p.append(hp_new)
            conv_p.append(cp_new)
            ssm_s.append(hs_new)
            conv_s.append(cs_new)
        else:
            if layer == N_A:
                kp, vp = shared_kv(xp, kv_norm_w, w_kv)
                ks, vs = shared_kv(xs, kv_norm_w, w_kv)
            i = layer - N_A
            qp = (rmsnorm(xp, nw[0]) @ attn_w_q[i]).reshape(bp, -1, ATT_HEADS, ATT_HEAD_DIM)
            qs = (rmsnorm(xs, nw[0]) @ attn_w_q[i]).reshape(bs, -1, ATT_HEADS, ATT_HEAD_DIM)
            mp = prompt_band_attention(qp, kp, vp, attn_rel_bias[i]) @ attn_w_o[i]
            ms = sample_band_attention(qs, ks, vs, cache_k, cache_v, attn_rel_bias[i]) @ attn_w_o[i]
        xp = xp + rmsnorm(mp, nw[1])
        xs = xs + rmsnorm(ms, nw[1])
        xp = xp + rmsnorm(swiglu(rmsnorm(xp, nw[2]), ffn_w_in[layer], ffn_w_out[layer]), nw[3])
        xs = xs + rmsnorm(swiglu(rmsnorm(xs, nw[2]), ffn_w_in[layer], ffn_w_out[layer]), nw[3])
    rows_p = min(BAND_PAST, xp.shape[1])
    return (xp, xs, jnp.stack(ssm_p), jnp.stack(conv_p), kp[:, -rows_p:], vp[:, -rows_p:],
            jnp.stack(ssm_s), jnp.stack(conv_s), ks, vs)
```

```python
import functools

import jax
import jax.numpy as jnp
from jax import lax
from jax.experimental import pallas as pl
from jax.experimental.pallas import tpu as pltpu

F32 = jnp.float32
BF16 = jnp.bfloat16

EPS = 1e-6
CHUNK = 64
BAND_CHUNKS = 8
BAND = (BAND_CHUNKS + 1) * CHUNK
REL_CLIP = 256
SSM_HEAD_DIM = 64
SSM_GROUPS = 4
SSM_STATE = 128
CONV_W = 4
ATT_HEAD_DIM = 64
CARRY_ROWS = 8

VMEM_LIMIT_BYTES = 56 * 1024 * 1024
ROW_TILE = 512
COL_TILE = 512


def _cparams(*sem):
    return pltpu.CompilerParams(dimension_semantics=sem,
                                vmem_limit_bytes=VMEM_LIMIT_BYTES)


def _resident(shape):
    zeros = (0,) * len(shape)
    return pl.BlockSpec(shape, lambda *_: zeros, pipeline_mode=pl.Buffered(1))


def _rms(x, w):
    ms = jnp.mean(x * x, axis=-1, keepdims=True)
    return x * lax.rsqrt(ms + EPS) * w


def _sigmoid(x):
    return 1.0 / (1.0 + jnp.exp(-x))


def _softplus(x):
    return jnp.maximum(x, 0.0) + jnp.log(1.0 + jnp.exp(-jnp.abs(x)))


def _dot(a, b):
    return jnp.dot(a, b, preferred_element_type=F32)


def _dot_nt(a, b):
    return lax.dot_general(a, b, (((1,), (1,)), ((), ())), preferred_element_type=F32)


def _split3(x):
    hi = x.astype(BF16)
    r1 = x - hi.astype(F32)
    mid = r1.astype(BF16)
    lo = (r1 - mid.astype(F32)).astype(BF16)
    return hi, mid, lo


def _inproj_kernel(x_ref, nw_ref, wz_ref, wx_ref, wdt_ref, wdtT_ref,
                   z_ref, xbc_ref, dt_ref, dtT_ref):
    xn = _rms(x_ref[...], nw_ref[...]).astype(BF16)
    for w_ref, o_ref in ((wz_ref, z_ref), (wx_ref, xbc_ref)):
        n = w_ref.shape[1]
        for c0 in range(0, n, COL_TILE):
            o_ref[:, c0:c0 + COL_TILE] = _dot(xn, w_ref[:, c0:c0 + COL_TILE]).astype(o_ref.dtype)
    dt_ref[...] = _dot(xn, wdt_ref[...])
    dtT_ref[...] = _dot_nt(wdtT_ref[...], xn)


def _inproj(x, nw, wz, wx, wdt, wdtT):
    rows, d = x.shape
    tm = ROW_TILE
    nh = wdt.shape[1]
    return pl.pallas_call(
        _inproj_kernel,
        grid=(rows // tm,),
        in_specs=[pl.BlockSpec((tm, d), lambda i: (i, 0)),
                  _resident(nw.shape), _resident(wz.shape), _resident(wx.shape),
                  _resident(wdt.shape), _resident(wdtT.shape)],
        out_specs=[pl.BlockSpec((tm, wz.shape[1]), lambda i: (i, 0)),
                   pl.BlockSpec((tm, wx.shape[1]), lambda i: (i, 0)),
                   pl.BlockSpec((tm, nh), lambda i: (i, 0)),
                   pl.BlockSpec((nh, tm), lambda i: (0, i))],
        out_shape=[jax.ShapeDtypeStruct((rows, wz.shape[1]), F32),
                   jax.ShapeDtypeStruct((rows, wx.shape[1]), F32),
                   jax.ShapeDtypeStruct((rows, nh), F32),
                   jax.ShapeDtypeStruct((nh, rows), F32)],
        compiler_params=_cparams("parallel"),
        name="ssm_in_proj",
    )(x, nw, wz, wx, wdt, wdtT)


def _ssd_kernel(xbc_ref, z_ref, dt_ref, dtT_ref, h0_ref, c0_ref,
                convw_ref, convb_ref, dtb_ref, dtbT_ref, alog_ref, alogT_ref,
                dskip_ref, gnw_ref,
                y_ref, hout_ref,
                win_ref, act_ref, xsb_ref, bb_ref, cb_ref, yacc_ref, state_ref, *, q):
    c = pl.program_id(1)
    d_inner = y_ref.shape[2]
    gn = SSM_GROUPS * SSM_STATE
    heads = d_inner // SSM_HEAD_DIM
    hpg = heads // SSM_GROUPS
    p = SSM_HEAD_DIM
    n = SSM_STATE

    @pl.when(c == 0)
    def _():
        state_ref[...] = h0_ref[0]
        win_ref[0:CARRY_ROWS, :] = c0_ref[0]

    win_ref[CARRY_ROWS:CARRY_ROWS + q, :] = xbc_ref[0]
    conv = convb_ref[...]
    for k in range(CONV_W):
        r0 = CARRY_ROWS - (CONV_W - 1) + k
        conv = conv + win_ref[r0:r0 + q, :] * convw_ref[k:k + 1, :]
    win_ref[0:CARRY_ROWS, :] = win_ref[q:q + CARRY_ROWS, :]
    act = conv * _sigmoid(conv)
    act_ref[...] = act
    xsb_ref[...] = act[:, :d_inner].astype(BF16)
    bb_ref[...] = act[:, d_inner:d_inner + gn].astype(BF16)
    cb_ref[...] = act[:, d_inner + gn:].astype(BF16)

    dt = _softplus(dt_ref[...] + dtb_ref[...])
    dtT = _softplus(dtT_ref[0] + dtbT_ref[...])
    dA = dt * (-jnp.exp(alog_ref[...]))
    dAT = dtT * (-jnp.exp(alogT_ref[...]))
    row = lax.broadcasted_iota(jnp.int32, (q, q), 0)
    col = lax.broadcasted_iota(jnp.int32, (q, q), 1)
    causal = row >= col
    tril = jnp.where(causal, 1.0, 0.0).astype(BF16)
    triu = jnp.where(row <= col, 1.0, 0.0).astype(BF16)
    acs = sum(_dot(tril, piece) for piece in _split3(dA))
    acsT = sum(_dot(piece, triu) for piece in _split3(dAT))

    for g in range(SSM_GROUPS):
        bg = bb_ref[:, g * n:(g + 1) * n]
        cg = cb_ref[:, g * n:(g + 1) * n]
        cbm = _dot_nt(cg, bg)
        cg32 = act_ref[:, d_inner + gn + g * n:d_inner + gn + (g + 1) * n]
        bgT = act_ref[:, d_inner + g * n:d_inner + (g + 1) * n].T
        for j in range(hpg):
            h = g * hpg + j
            a_col = acs[:, h:h + 1]
            a_row = acsT[h:h + 1, :]
            dt_row = dtT[h:h + 1, :]
            a_last = acsT[h:h + 1, q - 1:q]
            decay = jnp.exp(jnp.where(causal, a_col - a_row, -jnp.inf))
            m = (cbm * decay * dt_row).astype(BF16)
            e = (jnp.exp(a_col) * cg32).astype(BF16)
            xs_h = xsb_ref[:, h * p:(h + 1) * p]
            s_h = state_ref[:, h * p:(h + 1) * p]
            yacc_ref[:, h * p:(h + 1) * p] = _dot(m, xs_h) + _dot(e, s_h.astype(BF16))
            w_row = dt_row * jnp.exp(a_last - a_row)
            btw = (bgT * w_row).astype(BF16)
            state_ref[:, h * p:(h + 1) * p] = s_h * jnp.exp(a_last) + _dot(btw, xs_h)

    gw = d_inner // SSM_GROUPS
    for g in range(SSM_GROUPS):
        sl = slice(g * gw, (g + 1) * gw)
        zg = z_ref[0, :, sl]
        yg = (yacc_ref[:, sl] + act_ref[:, sl] * dskip_ref[:, sl]) * (zg * _sigmoid(zg))
        yg = yg * lax.rsqrt(jnp.mean(yg * yg, axis=-1, keepdims=True) + EPS)
        y_ref[0, :, sl] = (yg * gnw_ref[:, sl]).astype(y_ref.dtype)

    @pl.when(c == pl.num_programs(1) - 1)
    def _():
        hout_ref[0] = state_ref[...]


def _ssd(xbc, z, dt, dtT, h0, c0, convw, convb, dtb, dtbT, alog, alogT, dskip, gnw, *, q):
    b, l, conv_dim = xbc.shape
    d_inner = z.shape[2]
    nh = dt.shape[1]
    nc = l // q
    kern = functools.partial(_ssd_kernel, q=q)
    return pl.pallas_call(
        kern,
        grid=(b, nc),
        in_specs=[pl.BlockSpec((1, q, conv_dim), lambda i, c: (i, c, 0)),
                  pl.BlockSpec((1, q, d_inner), lambda i, c: (i, c, 0)),
                  pl.BlockSpec((q, nh), lambda i, c: (i * nc + c, 0)),
                  pl.BlockSpec((1, nh, q), lambda i, c: (i * nc + c, 0, 0)),
                  pl.BlockSpec((1, SSM_STATE, d_inner), lambda i, c: (i, 0, 0)),
                  pl.BlockSpec((1, CARRY_ROWS, conv_dim), lambda i, c: (i, 0, 0)),
                  _resident(convw.shape), _resident(convb.shape),
                  _resident(dtb.shape), _resident(dtbT.shape),
                  _resident(alog.shape), _resident(alogT.shape),
                  _resident(dskip.shape), _resident(gnw.shape)],
        out_specs=[pl.BlockSpec((1, q, d_inner), lambda i, c: (i, c, 0)),
                   pl.BlockSpec((1, SSM_STATE, d_inner), lambda i, c: (i, 0, 0))],
        out_shape=[jax.ShapeDtypeStruct((b, l, d_inner), BF16),
                   jax.ShapeDtypeStruct((b, SSM_STATE, d_inner), F32)],
        scratch_shapes=[pltpu.VMEM((CARRY_ROWS + q, conv_dim), F32),
                        pltpu.VMEM((q, conv_dim), F32),
                        pltpu.VMEM((q, d_inner), BF16),
                        pltpu.VMEM((q, SSM_GROUPS * SSM_STATE), BF16),
                        pltpu.VMEM((q, SSM_GROUPS * SSM_STATE), BF16),
                        pltpu.VMEM((q, d_inner), F32),
                        pltpu.VMEM((SSM_STATE, d_inner), F32)],
        compiler_params=_cparams("parallel", "arbitrary"),
        name="ssd_mixer",
    )(xbc, z, dt, dtT, h0, c0, convw, convb, dtb, dtbT, alog, alogT, dskip, gnw)


def _resproj_kernel(a_ref, w_ref, nw_ref, x_ref, o_ref):
    y = _dot(a_ref[...], w_ref[...])
    o_ref[...] = x_ref[...] + _rms(y, nw_ref[...])


def _resproj(a, w, nw, x, name):
    rows, k = a.shape
    d = w.shape[1]
    tm = ROW_TILE
    return pl.pallas_call(
        _resproj_kernel,
        grid=(rows // tm,),
        in_specs=[pl.BlockSpec((tm, k), lambda i: (i, 0)),
                  _resident(w.shape), _resident(nw.shape),
                  pl.BlockSpec((tm, d), lambda i: (i, 0))],
        out_specs=pl.BlockSpec((tm, d), lambda i: (i, 0)),
        out_shape=jax.ShapeDtypeStruct((rows, d), F32),
        compiler_params=_cparams("parallel"),
        name=name,
    )(a, w, nw, x)


def _ffn_kernel(x_ref, nw_in_ref, wg_ref, wu_ref, wo_ref, nw_out_ref, o_ref,
                xn_ref, h_ref, *, hc):
    xn_ref[...] = _rms(x_ref[...], nw_in_ref[...]).astype(BF16)
    hidden = wg_ref.shape[1]
    for c0 in range(0, hidden, hc):
        gate = _dot(xn_ref[...], wg_ref[:, c0:c0 + hc])
        up = _dot(xn_ref[...], wu_ref[:, c0:c0 + hc])
        h_ref[:, c0:c0 + hc] = (gate * _sigmoid(gate) * up).astype(BF16)
    y = _dot(h_ref[...], wo_ref[...])
    o_ref[...] = x_ref[...] + _rms(y, nw_out_ref[...])


def _ffn(x, nw_in, wg, wu, wo, nw_out, name):
    rows, d = x.shape
    hidden = wg.shape[1]
    tm = ROW_TILE
    kern = functools.partial(_ffn_kernel, hc=256)
    return pl.pallas_call(
        kern,
        grid=(rows // tm,),
        in_specs=[pl.BlockSpec((tm, d), lambda i: (i, 0)),
                  _resident(nw_in.shape), _resident(wg.shape), _resident(wu.shape),
                  _resident(wo.shape), _resident(nw_out.shape)],
        out_specs=pl.BlockSpec((tm, d), lambda i: (i, 0)),
        out_shape=jax.ShapeDtypeStruct((rows, d), F32),
        scratch_shapes=[pltpu.VMEM((tm, d), BF16), pltpu.VMEM((tm, hidden), BF16)],
        compiler_params=_cparams("parallel"),
        name=name,
    )(x, nw_in, wg, wu, wo, nw_out)


def _qkv_kernel(x_ref, nwq_ref, nwkv_ref, wq_ref, wk_ref, wv_ref,
                q_ref, k_ref, v_ref, kf_ref, vf_ref, *, pad_blocks, tail_blocks):
    i = pl.program_id(1)
    nb = pl.num_programs(1)

    @pl.when(i < pad_blocks)
    def _():
        k_ref[...] = jnp.zeros_like(k_ref)
        v_ref[...] = jnp.zeros_like(v_ref)

    @pl.when(i >= pad_blocks)
    def _():
        x = x_ref[0]
        xh = x * lax.rsqrt(jnp.mean(x * x, axis=-1, keepdims=True) + EPS)
        xq = (xh * nwq_ref[...]).astype(BF16)
        xkv = (xh * nwkv_ref[...]).astype(BF16)
        d = wq_ref.shape[1]
        for c0 in range(0, d, COL_TILE):
            sl = slice(c0, c0 + COL_TILE)
            q_ref[0, :, sl] = _dot(xq, wq_ref[:, sl]).astype(q_ref.dtype)
            kk = _dot(xkv, wk_ref[:, sl])
            vv = _dot(xkv, wv_ref[:, sl])
            k_ref[0, :, sl] = kk.astype(k_ref.dtype)
            v_ref[0, :, sl] = vv.astype(v_ref.dtype)

            @pl.when(i >= nb - tail_blocks)
            def _():
                kf_ref[0, :, sl] = kk
                vf_ref[0, :, sl] = vv


def _qkv(x, nwq, nwkv, wq, wk, wv, *, tm, pad_rows, tail_rows):
    b, l, d = x.shape
    pad_blocks = pad_rows // tm
    nblk = l // tm
    tail_blocks = tail_rows // tm
    kern = functools.partial(_qkv_kernel, pad_blocks=pad_blocks, tail_blocks=tail_blocks)
    src = lambda bi, i: (bi, jnp.maximum(i - pad_blocks, 0), 0)
    tail = lambda bi, i: (bi, jnp.maximum(i - pad_blocks - (nblk - tail_blocks), 0), 0)
    return pl.pallas_call(
        kern,
        grid=(b, nblk + pad_blocks),
        in_specs=[pl.BlockSpec((1, tm, d), src),
                  _resident(nwq.shape), _resident(nwkv.shape),
                  _resident(wq.shape), _resident(wk.shape), _resident(wv.shape)],
        out_specs=[pl.BlockSpec((1, tm, d), src),
                   pl.BlockSpec((1, tm, d), lambda bi, i: (bi, i, 0)),
                   pl.BlockSpec((1, tm, d), lambda bi, i: (bi, i, 0)),
                   pl.BlockSpec((1, tm, d), tail),
                   pl.BlockSpec((1, tm, d), tail)],
        out_shape=[jax.ShapeDtypeStruct((b, l, d), BF16),
                   jax.ShapeDtypeStruct((b, l + pad_rows, d), BF16),
                   jax.ShapeDtypeStruct((b, l + pad_rows, d), BF16),
                   jax.ShapeDtypeStruct((b, tail_rows, d), F32),
                   jax.ShapeDtypeStruct((b, tail_rows, d), F32)],
        compiler_params=_cparams("parallel", "arbitrary"),
        name="attn_qkv_proj",
    )(x, nwq, nwkv, wq, wk, wv)


def _bias_kernel(u_ref, o_ref):
    heads, width = u_ref.shape
    for h in range(heads):
        x = jnp.broadcast_to(u_ref[h:h + 1, :], (CHUNK, width))
        r = pltpu.roll(x, width - CHUNK, 1, stride=1, stride_axis=0)
        o_ref[h] = r[:, :BAND]


def _bias_table(rel_bias):
    heads = rel_bias.shape[0]
    width = BAND + CHUNK
    far = jnp.broadcast_to(rel_bias[:, 2 * REL_CLIP:], (heads, width - (REL_CLIP + CHUNK)))
    near = jnp.flip(rel_bias[:, REL_CLIP - CHUNK + 1:], axis=1)
    u = jnp.concatenate([far, near], axis=1)
    return pl.pallas_call(
        _bias_kernel,
        out_shape=jax.ShapeDtypeStruct((heads, CHUNK, BAND), F32),
        name="attn_rel_bias_table",
    )(u)


def _attn_kernel(q_ref, k_ref, v_ref, bias_ref, o_ref, *, first_chunk):
    c = pl.program_id(1)
    start = pl.multiple_of(c * CHUNK, CHUNK)
    heads = bias_ref.shape[0]
    hd = ATT_HEAD_DIM
    j = lax.broadcasted_iota(jnp.int32, (CHUNK, BAND), 1)
    valid = j >= (BAND_CHUNKS - (c + first_chunk)) * CHUNK
    scale = hd ** -0.5
    for h in range(heads):
        sl = slice(h * hd, (h + 1) * hd)
        qh = q_ref[0, :, sl]
        kh = k_ref[0, pl.ds(start, BAND), sl]
        vh = v_ref[0, pl.ds(start, BAND), sl]
        s = _dot_nt(qh, kh) * scale + bias_ref[h]
        s = jnp.where(valid, s, -jnp.inf)
        s = s - jnp.max(s, axis=-1, keepdims=True)
        pr = jnp.exp(s)
        denom = jnp.sum(pr, axis=-1, keepdims=True)
        o = _dot(pr.astype(BF16), vh) / denom
        o_ref[0, :, sl] = o.astype(o_ref.dtype)


def _attention(q, k, v, bias, *, first_chunk):
    b, l, d = q.shape
    lk = k.shape[1]
    kern = functools.partial(_attn_kernel, first_chunk=first_chunk)
    return pl.pallas_call(
        kern,
        grid=(b, l // CHUNK),
        in_specs=[pl.BlockSpec((1, CHUNK, d), lambda i, c: (i, c, 0)),
                  pl.BlockSpec((1, lk, d), lambda i, c: (i, 0, 0)),
                  pl.BlockSpec((1, lk, d), lambda i, c: (i, 0, 0)),
                  _resident(bias.shape)],
        out_specs=pl.BlockSpec((1, CHUNK, d), lambda i, c: (i, c, 0)),
        out_shape=jax.ShapeDtypeStruct((b, l, d), BF16),
        compiler_params=_cparams("parallel", "arbitrary"),
        name="band_attention",
    )(q, k, v, bias)


def _row(v):
    return v.reshape(1, -1).astype(F32)


def _ssd_layer(x3, h0, conv_prev, p, nw, *, q):
    b, l, d = x3.shape
    rows = b * l
    d_inner = p["wz"].shape[1]
    conv_dim = p["wx"].shape[1]
    x = x3.reshape(rows, d)
    z, xbc, dt, dtT = _inproj(x, _row(nw[0]), p["wz"], p["wx"], p["wdt"], p["wdtT"])
    xbc3 = xbc.reshape(b, l, conv_dim)
    h0t = jnp.transpose(h0, (0, 3, 1, 2)).reshape(b, SSM_STATE, d_inner)
    c0 = jnp.pad(conv_prev, ((0, 0), (CARRY_ROWS - (CONV_W - 1), 0), (0, 0)))
    dtT = jnp.transpose(dtT.reshape(-1, rows // q, q), (1, 0, 2))
    y, hT = _ssd(xbc3, z.reshape(b, l, d_inner), dt, dtT, h0t, c0,
                 p["convw"], p["convb"], p["dtb"], p["dtbT"], p["alog"], p["alogT"],
                 p["dskip"], p["gnw"], q=q)
    h_new = jnp.transpose(hT.reshape(b, SSM_STATE, -1, SSM_HEAD_DIM), (0, 2, 3, 1))
    new_conv = xbc3[:, l - (CONV_W - 1):]
    x1 = _resproj(y.reshape(rows, d_inner), p["wout"], _row(nw[1]), x, "ssm_out_proj")
    return x1.reshape(b, l, d), h_new, new_conv


def _ffn_layer(x3, p, nw, name):
    b, l, d = x3.shape
    out = _ffn(x3.reshape(b * l, d), _row(nw[2]), p["wg"], p["wu"], p["wo"], _row(nw[3]), name)
    return out.reshape(b, l, d)


def kernel(x_prompt, x_sample, state_ssm, state_conv, cache_k, cache_v, norm_w,
           ssm_w_in, ssm_conv_w, ssm_conv_b, ssm_dt_bias, ssm_A_log, ssm_D, ssm_norm_w, ssm_w_out,
           kv_norm_w, w_kv, attn_w_q, attn_rel_bias, attn_w_o, ffn_w_in, ffn_w_out):
    bp, lp, d = x_prompt.shape
    bs, ls, _ = x_sample.shape
    n_a = ssm_w_in.shape[0]
    depth = norm_w.shape[0]
    d_inner = ssm_w_out.shape[1]
    heads = ssm_dt_bias.shape[1]
    conv_dim = ssm_conv_w.shape[2]
    att_dim = attn_w_q.shape[2]
    att_heads = att_dim // ATT_HEAD_DIM
    hidden = ffn_w_out.shape[1]

    xp, xs = x_prompt, x_sample
    ssm_p, conv_p, ssm_s, conv_s = [], [], [], []
    outs_kv = None
    for layer in range(depth):
        nw = norm_w[layer]
        fp = {"wg": ffn_w_in[layer][:, :hidden].astype(BF16),
              "wu": ffn_w_in[layer][:, hidden:].astype(BF16),
              "wo": ffn_w_out[layer].astype(BF16)}
        if layer < n_a:
            a = layer
            w_in = ssm_w_in[a]
            wdt = w_in[:, d_inner + conv_dim:]
            sp = {"wz": w_in[:, :d_inner].astype(BF16),
                  "wx": w_in[:, d_inner:d_inner + conv_dim].astype(BF16),
                  "wdt": wdt.astype(BF16), "wdtT": wdt.T.astype(BF16),
                  "convw": ssm_conv_w[a], "convb": _row(ssm_conv_b[a]),
                  "dtb": _row(ssm_dt_bias[a]), "dtbT": ssm_dt_bias[a].reshape(-1, 1),
                  "alog": _row(ssm_A_log[a]), "alogT": ssm_A_log[a].reshape(-1, 1),
                  "dskip": _row(jnp.repeat(ssm_D[a], SSM_HEAD_DIM)),
                  "gnw": _row(ssm_norm_w[a]), "wout": ssm_w_out[a].astype(BF16)}
            h0 = jnp.zeros((bp, heads, SSM_HEAD_DIM, SSM_STATE), F32)
            c0 = jnp.zeros((bp, CONV_W - 1, conv_dim), F32)
            xp, hp_new, cp_new = _ssd_layer(xp, h0, c0, sp, nw, q=128)
            xs, hs_new, cs_new = _ssd_layer(xs, state_ssm[a], state_conv[a], sp, nw, q=ls)
            ssm_p.append(hp_new)
            conv_p.append(cp_new)
            ssm_s.append(hs_new)
            conv_s.append(cs_new)
        else:
            i = layer - n_a
            wq = attn_w_q[i].astype(BF16)
            wk = w_kv[:, :att_dim].astype(BF16)
            wv = w_kv[:, att_dim:].astype(BF16)
            wo = attn_w_o[i].astype(BF16)
            bias = _bias_table(attn_rel_bias[i])
            rows_p = min(BAND_CHUNKS * CHUNK, lp)
            qp, kpb, vpb, kpf, vpf = _qkv(xp, _row(nw[0]), _row(kv_norm_w), wq, wk, wv,
                                          tm=ROW_TILE, pad_rows=BAND_CHUNKS * CHUNK,
                                          tail_rows=rows_p)
            ap = _attention(qp, kpb, vpb, bias, first_chunk=0)
            xp = _resproj(ap.reshape(bp * lp, att_dim), wo, _row(nw[1]),
                          xp.reshape(bp * lp, d), "attn_out_proj").reshape(bp, lp, d)
            qs, ksb, vsb, ksf, vsf = _qkv(xs, _row(nw[0]), _row(kv_norm_w), wq, wk, wv,
                                          tm=ls, pad_rows=0, tail_rows=ls)
            r = cache_k.shape[1]
            kband = jnp.concatenate([cache_k.reshape(bs, r, att_dim).astype(BF16), ksb], axis=1)
            vband = jnp.concatenate([cache_v.reshape(bs, r, att_dim).astype(BF16), vsb], axis=1)
            a_s = _attention(qs, kband, vband, bias, first_chunk=BAND_CHUNKS)
            xs = _resproj(a_s.reshape(bs * ls, att_dim), wo, _row(nw[1]),
                          xs.reshape(bs * ls, d), "attn_out_proj").reshape(bs, ls, d)
            if outs_kv is None:
                outs_kv = (kpf.reshape(bp, rows_p, att_heads, ATT_HEAD_DIM),
                           vpf.reshape(bp, rows_p, att_heads, ATT_HEAD_DIM),
                           ksf.reshape(bs, ls, att_heads, ATT_HEAD_DIM),
                           vsf.reshape(bs, ls, att_heads, ATT_HEAD_DIM))
        xp = _ffn_layer(xp, fp, nw, "ffn_prompt")
        xs = _ffn_layer(xs, fp, nw, "ffn_sample")
    kp_out, vp_out, ks_out, vs_out = outs_kv
    return (xp, xs, jnp.stack(ssm_p), jnp.stack(conv_p), kp_out, vp_out,
            jnp.stack(ssm_s), jnp.stack(conv_s), ks_out, vs_out)
```

```python
import functools

import jax
import jax.numpy as jnp
from jax import lax
from jax.experimental import pallas as pl
from jax.experimental.pallas import tpu as pltpu

F32 = jnp.float32
BF16 = jnp.bfloat16

EPS = 1e-6
CHUNK = 64
BAND_CHUNKS = 8
BAND = (BAND_CHUNKS + 1) * CHUNK
REL_CLIP = 256
SSM_HEAD_DIM = 64
SSM_GROUPS = 4
SSM_STATE = 128
CONV_W = 4
ATT_HEAD_DIM = 64
LOG2E = 1.4426950408889634
Q_SCALE = ATT_HEAD_DIM ** -0.5 * LOG2E
CARRY_ROWS = 8

VMEM_LIMIT_BYTES = 56 * 1024 * 1024
ROW_TILE = 512
COL_TILE = 512


def _cparams(*sem):
    return pltpu.CompilerParams(dimension_semantics=sem,
                                vmem_limit_bytes=VMEM_LIMIT_BYTES)


def _resident(shape):
    zeros = (0,) * len(shape)
    return pl.BlockSpec(shape, lambda *_: zeros, pipeline_mode=pl.Buffered(1))


def _rms(x, w):
    ms = jnp.mean(x * x, axis=-1, keepdims=True)
    return x * lax.rsqrt(ms + EPS) * w


def _sigmoid(x):
    return 1.0 / (1.0 + jnp.exp(-x))


def _softplus(x):
    return jnp.maximum(x, 0.0) + jnp.log(1.0 + jnp.exp(-jnp.abs(x)))


def _dot(a, b):
    return jnp.dot(a, b, preferred_element_type=F32)


def _dot_nt(a, b):
    return lax.dot_general(a, b, (((1,), (1,)), ((), ())), preferred_element_type=F32)


def _split3(x):
    hi = x.astype(BF16)
    r1 = x - hi.astype(F32)
    mid = r1.astype(BF16)
    lo = (r1 - mid.astype(F32)).astype(BF16)
    return hi, mid, lo


def _inproj_kernel(x_ref, nw_ref, wz_ref, wx_ref, wdt_ref, wdtT_ref,
                   z_ref, xbc_ref, dt_ref, dtT_ref):
    xn = _rms(x_ref[...], nw_ref[...]).astype(BF16)
    for w_ref, o_ref in ((wz_ref, z_ref), (wx_ref, xbc_ref)):
        n = w_ref.shape[1]
        for c0 in range(0, n, COL_TILE):
            o_ref[:, c0:c0 + COL_TILE] = _dot(xn, w_ref[:, c0:c0 + COL_TILE]).astype(o_ref.dtype)
    dt_ref[...] = _dot(xn, wdt_ref[...])
    dtT_ref[...] = _dot_nt(wdtT_ref[...], xn)


def _inproj(x, nw, wz, wx, wdt, wdtT):
    rows, d = x.shape
    tm = ROW_TILE
    nh = wdt.shape[1]
    return pl.pallas_call(
        _inproj_kernel,
        grid=(rows // tm,),
        in_specs=[pl.BlockSpec((tm, d), lambda i: (i, 0)),
                  _resident(nw.shape), _resident(wz.shape), _resident(wx.shape),
                  _resident(wdt.shape), _resident(wdtT.shape)],
        out_specs=[pl.BlockSpec((tm, wz.shape[1]), lambda i: (i, 0)),
                   pl.BlockSpec((tm, wx.shape[1]), lambda i: (i, 0)),
                   pl.BlockSpec((tm, nh), lambda i: (i, 0)),
                   pl.BlockSpec((nh, tm), lambda i: (0, i))],
        out_shape=[jax.ShapeDtypeStruct((rows, wz.shape[1]), F32),
                   jax.ShapeDtypeStruct((rows, wx.shape[1]), F32),
                   jax.ShapeDtypeStruct((rows, nh), F32),
                   jax.ShapeDtypeStruct((nh, rows), F32)],
        compiler_params=_cparams("parallel"),
        name="ssm_in_proj",
    )(x, nw, wz, wx, wdt, wdtT)


def _ssd_kernel(xbc_ref, z_ref, dt_ref, dtT_ref, h0_ref, c0_ref,
                convw_ref, convb_ref, dtb_ref, dtbT_ref, alog_ref, alogT_ref,
                dskip_ref, gnw_ref,
                y_ref, hout_ref,
                win_ref, act_ref, xsb_ref, bb_ref, cb_ref, yacc_ref, state_ref, *, q):
    c = pl.program_id(1)
    d_inner = y_ref.shape[2]
    gn = SSM_GROUPS * SSM_STATE
    heads = d_inner // SSM_HEAD_DIM
    hpg = heads // SSM_GROUPS
    p = SSM_HEAD_DIM
    n = SSM_STATE

    @pl.when(c == 0)
    def _():
        state_ref[...] = h0_ref[0]
        win_ref[0:CARRY_ROWS, :] = c0_ref[0]

    win_ref[CARRY_ROWS:CARRY_ROWS + q, :] = xbc_ref[0]
    conv = convb_ref[...]
    for k in range(CONV_W):
        r0 = CARRY_ROWS - (CONV_W - 1) + k
        conv = conv + win_ref[r0:r0 + q, :] * convw_ref[k:k + 1, :]
    win_ref[0:CARRY_ROWS, :] = win_ref[q:q + CARRY_ROWS, :]
    act = conv * _sigmoid(conv)
    act_ref[...] = act
    xsb_ref[...] = act[:, :d_inner].astype(BF16)
    bb_ref[...] = act[:, d_inner:d_inner + gn].astype(BF16)
    cb_ref[...] = act[:, d_inner + gn:].astype(BF16)

    dt = _softplus(dt_ref[...] + dtb_ref[...])
    dtT = _softplus(dtT_ref[0] + dtbT_ref[...])
    dA = dt * (-jnp.exp(alog_ref[...]))
    dAT = dtT * (-jnp.exp(alogT_ref[...]))
    row = lax.broadcasted_iota(jnp.int32, (q, q), 0)
    col = lax.broadcasted_iota(jnp.int32, (q, q), 1)
    causal = row >= col
    tril = jnp.where(causal, 1.0, 0.0).astype(BF16)
    triu = jnp.where(row <= col, 1.0, 0.0).astype(BF16)
    acs = sum(_dot(tril, piece) for piece in _split3(dA))
    acsT = sum(_dot(piece, triu) for piece in _split3(dAT))

    for g in range(SSM_GROUPS):
        bg = bb_ref[:, g * n:(g + 1) * n]
        cg = cb_ref[:, g * n:(g + 1) * n]
        cbm = _dot_nt(cg, bg)
        cg32 = act_ref[:, d_inner + gn + g * n:d_inner + gn + (g + 1) * n]
        bgT = act_ref[:, d_inner + g * n:d_inner + (g + 1) * n].T
        for j in range(hpg):
            h = g * hpg + j
            a_col = acs[:, h:h + 1]
            a_row = acsT[h:h + 1, :]
            dt_row = dtT[h:h + 1, :]
            a_last = acsT[h:h + 1, q - 1:q]
            decay = jnp.exp(jnp.where(causal, a_col - a_row, -jnp.inf))
            m = (cbm * decay * dt_row).astype(BF16)
            e = (jnp.exp(a_col) * cg32).astype(BF16)
            xs_h = xsb_ref[:, h * p:(h + 1) * p]
            s_h = state_ref[:, h * p:(h + 1) * p]
            yacc_ref[:, h * p:(h + 1) * p] = _dot(m, xs_h) + _dot(e, s_h.astype(BF16))
            w_row = dt_row * jnp.exp(a_last - a_row)
            btw = (bgT * w_row).astype(BF16)
            state_ref[:, h * p:(h + 1) * p] = s_h * jnp.exp(a_last) + _dot(btw, xs_h)

    gw = d_inner // SSM_GROUPS
    for g in range(SSM_GROUPS):
        sl = slice(g * gw, (g + 1) * gw)
        zg = z_ref[0, :, sl]
        yg = (yacc_ref[:, sl] + act_ref[:, sl] * dskip_ref[:, sl]) * (zg * _sigmoid(zg))
        yg = yg * lax.rsqrt(jnp.mean(yg * yg, axis=-1, keepdims=True) + EPS)
        y_ref[0, :, sl] = (yg * gnw_ref[:, sl]).astype(y_ref.dtype)

    @pl.when(c == pl.num_programs(1) - 1)
    def _():
        hout_ref[0] = state_ref[...]


def _ssd(xbc, z, dt, dtT, h0, c0, convw, convb, dtb, dtbT, alog, alogT, dskip, gnw, *, q):
    b, l, conv_dim = xbc.shape
    d_inner = z.shape[2]
    nh = dt.shape[1]
    nc = l // q
    kern = functools.partial(_ssd_kernel, q=q)
    return pl.pallas_call(
        kern,
        grid=(b, nc),
        in_specs=[pl.BlockSpec((1, q, conv_dim), lambda i, c: (i, c, 0)),
                  pl.BlockSpec((1, q, d_inner), lambda i, c: (i, c, 0)),
                  pl.BlockSpec((q, nh), lambda i, c: (i * nc + c, 0)),
                  pl.BlockSpec((1, nh, q), lambda i, c: (i * nc + c, 0, 0)),
                  pl.BlockSpec((1, SSM_STATE, d_inner), lambda i, c: (i, 0, 0)),
                  pl.BlockSpec((1, CARRY_ROWS, conv_dim), lambda i, c: (i, 0, 0)),
                  _resident(convw.shape), _resident(convb.shape),
                  _resident(dtb.shape), _resident(dtbT.shape),
                  _resident(alog.shape), _resident(alogT.shape),
                  _resident(dskip.shape), _resident(gnw.shape)],
        out_specs=[pl.BlockSpec((1, q, d_inner), lambda i, c: (i, c, 0)),
                   pl.BlockSpec((1, SSM_STATE, d_inner), lambda i, c: (i, 0, 0))],
        out_shape=[jax.ShapeDtypeStruct((b, l, d_inner), BF16),
                   jax.ShapeDtypeStruct((b, SSM_STATE, d_inner), F32)],
        scratch_shapes=[pltpu.VMEM((CARRY_ROWS + q, conv_dim), F32),
                        pltpu.VMEM((q, conv_dim), F32),
                        pltpu.VMEM((q, d_inner), BF16),
                        pltpu.VMEM((q, SSM_GROUPS * SSM_STATE), BF16),
                        pltpu.VMEM((q, SSM_GROUPS * SSM_STATE), BF16),
                        pltpu.VMEM((q, d_inner), F32),
                        pltpu.VMEM((SSM_STATE, d_inner), F32)],
        compiler_params=_cparams("parallel", "arbitrary"),
        name="ssd_mixer",
    )(xbc, z, dt, dtT, h0, c0, convw, convb, dtb, dtbT, alog, alogT, dskip, gnw)


def _resproj_kernel(a_ref, w_ref, nw_ref, x_ref, o_ref):
    y = _dot(a_ref[...], w_ref[...])
    o_ref[...] = x_ref[...] + _rms(y, nw_ref[...])


def _resproj(a, w, nw, x, name):
    rows, k = a.shape
    d = w.shape[1]
    tm = ROW_TILE
    return pl.pallas_call(
        _resproj_kernel,
        grid=(rows // tm,),
        in_specs=[pl.BlockSpec((tm, k), lambda i: (i, 0)),
                  _resident(w.shape), _resident(nw.shape),
                  pl.BlockSpec((tm, d), lambda i: (i, 0))],
        out_specs=pl.BlockSpec((tm, d), lambda i: (i, 0)),
        out_shape=jax.ShapeDtypeStruct((rows, d), F32),
        compiler_params=_cparams("parallel"),
        name=name,
    )(a, w, nw, x)


def _ffn_kernel(x_ref, nw_in_ref, wg_ref, wu_ref, wo_ref, nw_out_ref, o_ref,
                xn_ref, h_ref, *, hc):
    xn_ref[...] = _rms(x_ref[...], nw_in_ref[...]).astype(BF16)
    hidden = wg_ref.shape[1]
    for c0 in range(0, hidden, hc):
        gate = _dot(xn_ref[...], wg_ref[:, c0:c0 + hc])
        up = _dot(xn_ref[...], wu_ref[:, c0:c0 + hc])
        h_ref[:, c0:c0 + hc] = (gate * _sigmoid(gate) * up).astype(BF16)
    y = _dot(h_ref[...], wo_ref[...])
    o_ref[...] = x_ref[...] + _rms(y, nw_out_ref[...])


def _ffn(x, nw_in, wg, wu, wo, nw_out, name):
    rows, d = x.shape
    hidden = wg.shape[1]
    tm = ROW_TILE
    kern = functools.partial(_ffn_kernel, hc=256)
    return pl.pallas_call(
        kern,
        grid=(rows // tm,),
        in_specs=[pl.BlockSpec((tm, d), lambda i: (i, 0)),
                  _resident(nw_in.shape), _resident(wg.shape), _resident(wu.shape),
                  _resident(wo.shape), _resident(nw_out.shape)],
        out_specs=pl.BlockSpec((tm, d), lambda i: (i, 0)),
        out_shape=jax.ShapeDtypeStruct((rows, d), F32),
        scratch_shapes=[pltpu.VMEM((tm, d), BF16), pltpu.VMEM((tm, hidden), BF16)],
        compiler_params=_cparams("parallel"),
        name=name,
    )(x, nw_in, wg, wu, wo, nw_out)


def _qkv_kernel(x_ref, nwq_ref, nwkv_ref, wq_ref, wk_ref, wv_ref,
                q_ref, k_ref, v_ref, kf_ref, vf_ref, *, pad_blocks, tail_blocks):
    i = pl.program_id(1)
    nb = pl.num_programs(1)

    @pl.when(i < pad_blocks)
    def _():
        k_ref[...] = jnp.zeros_like(k_ref)
        v_ref[...] = jnp.zeros_like(v_ref)

    @pl.when(i >= pad_blocks)
    def _():
        x = x_ref[0]
        xh = x * lax.rsqrt(jnp.mean(x * x, axis=-1, keepdims=True) + EPS)
        xq = (xh * nwq_ref[...]).astype(BF16)
        xkv = (xh * nwkv_ref[...]).astype(BF16)
        d = wq_ref.shape[1]
        for c0 in range(0, d, COL_TILE):
            sl = slice(c0, c0 + COL_TILE)
            q_ref[0, :, sl] = (_dot(xq, wq_ref[:, sl]) * Q_SCALE).astype(q_ref.dtype)
            kk = _dot(xkv, wk_ref[:, sl])
            vv = _dot(xkv, wv_ref[:, sl])
            k_ref[0, :, sl] = kk.astype(k_ref.dtype)
            v_ref[0, :, sl] = vv.astype(v_ref.dtype)

            @pl.when(i >= nb - tail_blocks)
            def _():
                kf_ref[0, :, sl] = kk
                vf_ref[0, :, sl] = vv


def _qkv(x, nwq, nwkv, wq, wk, wv, *, tm, pad_rows, tail_rows):
    b, l, d = x.shape
    pad_blocks = pad_rows // tm
    nblk = l // tm
    tail_blocks = tail_rows // tm
    kern = functools.partial(_qkv_kernel, pad_blocks=pad_blocks, tail_blocks=tail_blocks)
    src = lambda bi, i: (bi, jnp.maximum(i - pad_blocks, 0), 0)
    tail = lambda bi, i: (bi, jnp.maximum(i - pad_blocks - (nblk - tail_blocks), 0), 0)
    return pl.pallas_call(
        kern,
        grid=(b, nblk + pad_blocks),
        in_specs=[pl.BlockSpec((1, tm, d), src),
                  _resident(nwq.shape), _resident(nwkv.shape),
                  _resident(wq.shape), _resident(wk.shape), _resident(wv.shape)],
        out_specs=[pl.BlockSpec((1, tm, d), src),
                   pl.BlockSpec((1, tm, d), lambda bi, i: (bi, i, 0)),
                   pl.BlockSpec((1, tm, d), lambda bi, i: (bi, i, 0)),
                   pl.BlockSpec((1, tm, d), tail),
                   pl.BlockSpec((1, tm, d), tail)],
        out_shape=[jax.ShapeDtypeStruct((b, l, d), BF16),
                   jax.ShapeDtypeStruct((b, l + pad_rows, d), BF16),
                   jax.ShapeDtypeStruct((b, l + pad_rows, d), BF16),
                   jax.ShapeDtypeStruct((b, tail_rows, d), F32),
                   jax.ShapeDtypeStruct((b, tail_rows, d), F32)],
        compiler_params=_cparams("parallel", "arbitrary"),
        name="attn_qkv_proj",
    )(x, nwq, nwkv, wq, wk, wv)


def _bias_kernel(u_ref, o_ref):
    heads, width = u_ref.shape
    for h in range(heads):
        x = jnp.broadcast_to(u_ref[h:h + 1, :] * LOG2E, (CHUNK, width))
        r = pltpu.roll(x, width - CHUNK, 1, stride=1, stride_axis=0)
        o_ref[h] = r[:, :BAND]


def _bias_table(rel_bias):
    heads = rel_bias.shape[0]
    width = BAND + CHUNK
    far = jnp.broadcast_to(rel_bias[:, 2 * REL_CLIP:], (heads, width - (REL_CLIP + CHUNK)))
    near = jnp.flip(rel_bias[:, REL_CLIP - CHUNK + 1:], axis=1)
    u = jnp.concatenate([far, near], axis=1)
    return pl.pallas_call(
        _bias_kernel,
        out_shape=jax.ShapeDtypeStruct((heads, CHUNK, BAND), F32),
        name="attn_rel_bias_table",
    )(u)


def _attn_kernel(q_ref, k_ref, v_ref, bias_ref, o_ref, kbuf_ref, vbuf_ref, *, first_chunk):
    c = pl.program_id(1)

    @pl.when(c == 0)
    def _():
        kbuf_ref[...] = k_ref[0]
        vbuf_ref[...] = v_ref[0]

    start = pl.multiple_of(c * CHUNK, CHUNK)
    pairs = bias_ref.shape[0]
    pw = 2 * ATT_HEAD_DIM
    left = lax.broadcasted_iota(jnp.int32, (CHUNK, pw), 1) < ATT_HEAD_DIM

    def body(masked):
        if masked:
            j = lax.broadcasted_iota(jnp.int32, (2 * CHUNK, BAND), 1)
            valid = j >= (BAND_CHUNKS - (c + first_chunk)) * CHUNK

        def scores(hp):
            sl = slice(hp * pw, (hp + 1) * pw)
            qp = q_ref[0, :, sl]
            zero = jnp.zeros_like(qp)
            q2 = jnp.concatenate([jnp.where(left, qp, zero), jnp.where(left, zero, qp)], axis=0)
            kp = kbuf_ref[pl.ds(start, BAND), sl]
            s = _dot_nt(q2, kp) + bias_ref[hp]
            return jnp.where(valid, s, -jnp.inf) if masked else s

        def softmax(s):
            pr = jnp.exp2(s - jnp.max(s, axis=-1, keepdims=True))
            return pr.astype(BF16), jnp.sum(pr, axis=-1, keepdims=True)

        def weighted(hp, pr):
            return _dot(pr, vbuf_ref[pl.ds(start, BAND), hp * pw:(hp + 1) * pw])

        def finish(hp, r, denom):
            r = r / denom
            o_ref[0, :, hp * pw:(hp + 1) * pw] = jnp.where(left, r[:CHUNK], r[CHUNK:]).astype(o_ref.dtype)

        s, pr, den, acc = {}, {}, {}, {}
        for t in range(pairs + 3):
            if t < pairs:
                s[t] = scores(t)
            if 0 <= t - 2 < pairs:
                acc[t - 2] = weighted(t - 2, pr.pop(t - 2))
            if 0 <= t - 1 < pairs:
                pr[t - 1], den[t - 1] = softmax(s.pop(t - 1))
            if 0 <= t - 3 < pairs:
                finish(t - 3, acc.pop(t - 3), den.pop(t - 3))

    if first_chunk >= BAND_CHUNKS:
        body(False)
    else:
        band_full = c + first_chunk >= BAND_CHUNKS
        pl.when(band_full)(lambda: body(False))
        pl.when(jnp.logical_not(band_full))(lambda: body(True))


def _attention(q, k, v, bias, *, first_chunk):
    b, l, d = q.shape
    lk = k.shape[1]
    kern = functools.partial(_attn_kernel, first_chunk=first_chunk)
    return pl.pallas_call(
        kern,
        grid=(b, l // CHUNK),
        in_specs=[pl.BlockSpec((1, CHUNK, d), lambda i, c: (i, c, 0)),
                  pl.BlockSpec((1, lk, d), lambda i, c: (i, 0, 0)),
                  pl.BlockSpec((1, lk, d), lambda i, c: (i, 0, 0)),
                  _resident(bias.shape)],
        out_specs=pl.BlockSpec((1, CHUNK, d), lambda i, c: (i, c, 0)),
        out_shape=jax.ShapeDtypeStruct((b, l, d), BF16),
        scratch_shapes=[pltpu.VMEM((lk, d), BF16), pltpu.VMEM((lk, d), BF16)],
        compiler_params=_cparams("parallel", "arbitrary"),
        name="band_attention",
    )(q, k, v, bias)


def _row(v):
    return v.reshape(1, -1).astype(F32)


def _ssd_layer(x3, h0, conv_prev, p, nw, *, q):
    b, l, d = x3.shape
    rows = b * l
    d_inner = p["wz"].shape[1]
    conv_dim = p["wx"].shape[1]
    x = x3.reshape(rows, d)
    z, xbc, dt, dtT = _inproj(x, _row(nw[0]), p["wz"], p["wx"], p["wdt"], p["wdtT"])
    xbc3 = xbc.reshape(b, l, conv_dim)
    h0t = jnp.transpose(h0, (0, 3, 1, 2)).reshape(b, SSM_STATE, d_inner)
    c0 = jnp.pad(conv_prev, ((0, 0), (CARRY_ROWS - (CONV_W - 1), 0), (0, 0)))
    dtT = jnp.transpose(dtT.reshape(-1, rows // q, q), (1, 0, 2))
    y, hT = _ssd(xbc3, z.reshape(b, l, d_inner), dt, dtT, h0t, c0,
                 p["convw"], p["convb"], p["dtb"], p["dtbT"], p["alog"], p["alogT"],
                 p["dskip"], p["gnw"], q=q)
    h_new = jnp.transpose(hT.reshape(b, SSM_STATE, -1, SSM_HEAD_DIM), (0, 2, 3, 1))
    new_conv = xbc3[:, l - (CONV_W - 1):]
    x1 = _resproj(y.reshape(rows, d_inner), p["wout"], _row(nw[1]), x, "ssm_out_proj")
    return x1.reshape(b, l, d), h_new, new_conv


def _ffn_layer(x3, p, nw, name):
    b, l, d = x3.shape
    out = _ffn(x3.reshape(b * l, d), _row(nw[2]), p["wg"], p["wu"], p["wo"], _row(nw[3]), name)
    return out.reshape(b, l, d)


def kernel(x_prompt, x_sample, state_ssm, state_conv, cache_k, cache_v, norm_w,
           ssm_w_in, ssm_conv_w, ssm_conv_b, ssm_dt_bias, ssm_A_log, ssm_D, ssm_norm_w, ssm_w_out,
           kv_norm_w, w_kv, attn_w_q, attn_rel_bias, attn_w_o, ffn_w_in, ffn_w_out):
    bp, lp, d = x_prompt.shape
    bs, ls, _ = x_sample.shape
    n_a = ssm_w_in.shape[0]
    depth = norm_w.shape[0]
    d_inner = ssm_w_out.shape[1]
    heads = ssm_dt_bias.shape[1]
    conv_dim = ssm_conv_w.shape[2]
    att_dim = attn_w_q.shape[2]
    att_heads = att_dim // ATT_HEAD_DIM
    hidden = ffn_w_out.shape[1]

    xp, xs = x_prompt, x_sample
    ssm_p, conv_p, ssm_s, conv_s = [], [], [], []
    outs_kv = None
    for layer in range(depth):
        nw = norm_w[layer]
        fp = {"wg": ffn_w_in[layer][:, :hidden].astype(BF16),
              "wu": ffn_w_in[layer][:, hidden:].astype(BF16),
              "wo": ffn_w_out[layer].astype(BF16)}
        if layer < n_a:
            a = layer
            w_in = ssm_w_in[a]
            wdt = w_in[:, d_inner + conv_dim:]
            sp = {"wz": w_in[:, :d_inner].astype(BF16),
                  "wx": w_in[:, d_inner:d_inner + conv_dim].astype(BF16),
                  "wdt": wdt.astype(BF16), "wdtT": wdt.T.astype(BF16),
                  "convw": ssm_conv_w[a], "convb": _row(ssm_conv_b[a]),
                  "dtb": _row(ssm_dt_bias[a]), "dtbT": ssm_dt_bias[a].reshape(-1, 1),
                  "alog": _row(ssm_A_log[a]), "alogT": ssm_A_log[a].reshape(-1, 1),
                  "dskip": _row(jnp.repeat(ssm_D[a], SSM_HEAD_DIM)),
                  "gnw": _row(ssm_norm_w[a]), "wout": ssm_w_out[a].astype(BF16)}
            h0 = jnp.zeros((bp, heads, SSM_HEAD_DIM, SSM_STATE), F32)
            c0 = jnp.zeros((bp, CONV_W - 1, conv_dim), F32)
            xp, hp_new, cp_new = _ssd_layer(xp, h0, c0, sp, nw, q=128)
            xs, hs_new, cs_new = _ssd_layer(xs, state_ssm[a], state_conv[a], sp, nw, q=ls)
            ssm_p.append(hp_new)
            conv_p.append(cp_new)
            ssm_s.append(hs_new)
            conv_s.append(cs_new)
        else:
            i = layer - n_a
            wq = attn_w_q[i].astype(BF16)
            wk = w_kv[:, :att_dim].astype(BF16)
            wv = w_kv[:, att_dim:].astype(BF16)
            wo = attn_w_o[i].astype(BF16)
            bias = _bias_table(attn_rel_bias[i]).reshape(att_heads // 2, 2 * CHUNK, BAND)
            rows_p = min(BAND_CHUNKS * CHUNK, lp)
            qp, kpb, vpb, kpf, vpf = _qkv(xp, _row(nw[0]), _row(kv_norm_w), wq, wk, wv,
                                          tm=ROW_TILE, pad_rows=BAND_CHUNKS * CHUNK,
                                          tail_rows=rows_p)
            ap = _attention(qp, kpb, vpb, bias, first_chunk=0)
            xp = _resproj(ap.reshape(bp * lp, att_dim), wo, _row(nw[1]),
                          xp.reshape(bp * lp, d), "attn_out_proj").reshape(bp, lp, d)
            qs, ksb, vsb, ksf, vsf = _qkv(xs, _row(nw[0]), _row(kv_norm_w), wq, wk, wv,
                                          tm=ls, pad_rows=0, tail_rows=ls)
            r = cache_k.shape[1]
            kband = jnp.concatenate([cache_k.reshape(bs, r, att_dim).astype(BF16), ksb], axis=1)
            vband = jnp.concatenate([cache_v.reshape(bs, r, att_dim).astype(BF16), vsb], axis=1)
            a_s = _attention(qs, kband, vband, bias, first_chunk=BAND_CHUNKS)
            xs = _resproj(a_s.reshape(bs * ls, att_dim), wo, _row(nw[1]),
                          xs.reshape(bs * ls, d), "attn_out_proj").reshape(bs, ls, d)
            if outs_kv is None:
                outs_kv = (kpf.reshape(bp, rows_p, att_heads, ATT_HEAD_DIM),
                           vpf.reshape(bp, rows_p, att_heads, ATT_HEAD_DIM),
                           ksf.reshape(bs, ls, att_heads, ATT_HEAD_DIM),
                           vsf.reshape(bs, ls, att_heads, ATT_HEAD_DIM))
        xp = _ffn_layer(xp, fp, nw, "ffn_prompt")
        xs = _ffn_layer(xs, fp, nw, "ffn_sample")
    kp_out, vp_out, ks_out, vs_out = outs_kv
    return (xp, xs, jnp.stack(ssm_p), jnp.stack(conv_p), kp_out, vp_out,
            jnp.stack(ssm_s), jnp.stack(conv_s), ks_out, vs_out)
```

```python
import functools

import jax
import jax.numpy as jnp
from jax import lax
from jax.experimental import pallas as pl
from jax.experimental.pallas import tpu as pltpu

F32 = jnp.float32
BF16 = jnp.bfloat16

EPS = 1e-6
CHUNK = 64
BAND_CHUNKS = 8
BAND = (BAND_CHUNKS + 1) * CHUNK
REL_CLIP = 256
SSM_HEAD_DIM = 64
SSM_GROUPS = 4
SSM_STATE = 128
CONV_W = 4
ATT_HEAD_DIM = 64
LOG2E = 1.4426950408889634
Q_SCALE = ATT_HEAD_DIM ** -0.5 * LOG2E
CARRY_ROWS = 8

VMEM_LIMIT_BYTES = 56 * 1024 * 1024
ROW_TILE = 512
COL_TILE = 512


def _cparams(*sem):
    return pltpu.CompilerParams(dimension_semantics=sem,
                                vmem_limit_bytes=VMEM_LIMIT_BYTES)


def _resident(shape):
    zeros = (0,) * len(shape)
    return pl.BlockSpec(shape, lambda *_: zeros, pipeline_mode=pl.Buffered(1))


def _rms(x, w):
    ms = jnp.mean(x * x, axis=-1, keepdims=True)
    return x * lax.rsqrt(ms + EPS) * w


def _sigmoid(x):
    return 1.0 / (1.0 + jnp.exp2(x * -LOG2E))


def _softplus(x):
    return jnp.maximum(x, 0.0) + jnp.log(1.0 + jnp.exp(-jnp.abs(x)))


def _dot(a, b):
    return jnp.dot(a, b, preferred_element_type=F32)


def _dot_nt(a, b):
    return lax.dot_general(a, b, (((1,), (1,)), ((), ())), preferred_element_type=F32)


def _split3(x):
    hi = x.astype(BF16)
    r1 = x - hi.astype(F32)
    mid = r1.astype(BF16)
    lo = (r1 - mid.astype(F32)).astype(BF16)
    return hi, mid, lo


def _inproj_kernel(x_ref, nw_ref, wz_ref, wx_ref, wdt_ref, wdtT_ref,
                   z_ref, xbc_ref, dt_ref, dtT_ref):
    xn = _rms(x_ref[...], nw_ref[...]).astype(BF16)
    for w_ref, o_ref in ((wz_ref, z_ref), (wx_ref, xbc_ref)):
        n = w_ref.shape[1]
        for c0 in range(0, n, COL_TILE):
            o_ref[:, c0:c0 + COL_TILE] = _dot(xn, w_ref[:, c0:c0 + COL_TILE]).astype(o_ref.dtype)
    dt_ref[...] = _dot(xn, wdt_ref[...])
    dtT_ref[...] = _dot_nt(wdtT_ref[...], xn)


def _inproj(x, nw, wz, wx, wdt, wdtT):
    rows, d = x.shape
    tm = ROW_TILE
    nh = wdt.shape[1]
    return pl.pallas_call(
        _inproj_kernel,
        grid=(rows // tm,),
        in_specs=[pl.BlockSpec((tm, d), lambda i: (i, 0)),
                  _resident(nw.shape), _resident(wz.shape), _resident(wx.shape),
                  _resident(wdt.shape), _resident(wdtT.shape)],
        out_specs=[pl.BlockSpec((tm, wz.shape[1]), lambda i: (i, 0)),
                   pl.BlockSpec((tm, wx.shape[1]), lambda i: (i, 0)),
                   pl.BlockSpec((tm, nh), lambda i: (i, 0)),
                   pl.BlockSpec((nh, tm), lambda i: (0, i))],
        out_shape=[jax.ShapeDtypeStruct((rows, wz.shape[1]), F32),
                   jax.ShapeDtypeStruct((rows, wx.shape[1]), F32),
                   jax.ShapeDtypeStruct((rows, nh), F32),
                   jax.ShapeDtypeStruct((nh, rows), F32)],
        compiler_params=_cparams("parallel"),
        name="ssm_in_proj",
    )(x, nw, wz, wx, wdt, wdtT)


def _ssd_kernel(xbc_ref, z_ref, dt_ref, dtT_ref, h0_ref, c0_ref,
                convw_ref, convb_ref, dtb_ref, dtbT_ref, alog_ref, alogT_ref,
                dskip_ref, gnw_ref, expand_ref,
                y_ref, hout_ref,
                win_ref, act_ref, xsb_ref, bb_ref, cb_ref, cbm_ref, bT_ref, yacc_ref, state_ref,
                *, q):
    c = pl.program_id(1)
    d_inner = y_ref.shape[2]
    n = SSM_STATE
    gn = SSM_GROUPS * n
    lanes = 2 * SSM_HEAD_DIM
    pairs = d_inner // lanes
    pairs_per_group = pairs // SSM_GROUPS
    panels = win_ref.shape[0]
    b_off, c_off = d_inner, d_inner + gn

    @pl.when(c == 0)
    def _():
        state_ref[...] = h0_ref[0]
        for pn in range(panels):
            win_ref[pn, 0:CARRY_ROWS, :] = c0_ref[0, :, pn * lanes:(pn + 1) * lanes]

    for pn in range(panels):
        sl = slice(pn * lanes, (pn + 1) * lanes)
        win_ref[pn, CARRY_ROWS:CARRY_ROWS + q, :] = xbc_ref[0, :, sl]
        conv = convb_ref[:, sl]
        for k in range(CONV_W):
            r0 = CARRY_ROWS - (CONV_W - 1) + k
            conv = conv + win_ref[pn, r0:r0 + q, :] * convw_ref[k:k + 1, sl]
        win_ref[pn, 0:CARRY_ROWS, :] = win_ref[pn, q:q + CARRY_ROWS, :]
        act = conv * _sigmoid(conv)
        act_ref[:, sl] = act
        if pn * lanes < b_off:
            xsb_ref[:, sl] = act.astype(BF16)
        elif pn * lanes < c_off:
            bb_ref[:, pn * lanes - b_off:(pn + 1) * lanes - b_off] = act.astype(BF16)
        else:
            cb_ref[:, pn * lanes - c_off:(pn + 1) * lanes - c_off] = act.astype(BF16)

    dt = _softplus(dt_ref[...] + dtb_ref[...])
    dtT = _softplus(dtT_ref[0] + dtbT_ref[...])
    dA = dt * (-LOG2E * jnp.exp(alog_ref[...]))
    dAT = dtT * (-LOG2E * jnp.exp(alogT_ref[...]))
    row = lax.broadcasted_iota(jnp.int32, (q, q), 0)
    col = lax.broadcasted_iota(jnp.int32, (q, q), 1)
    causal = row >= col
    tril = jnp.where(causal, 1.0, 0.0).astype(BF16)
    triu = jnp.where(row <= col, 1.0, 0.0).astype(BF16)
    acs = sum(_dot(tril, piece) for piece in _split3(dA))
    acsT = sum(_dot(piece, triu) for piece in _split3(dAT))
    a_lastT = acsT[:, q - 1:q]
    row_term = acsT - jnp.log(dtT) * LOG2E
    w_rows = dtT * jnp.exp2(a_lastT - acsT)
    d_last = jnp.broadcast_to(jnp.exp2(acs[q - 1:q, :]), (8, acs.shape[1]))
    d_lanes = sum(_dot(piece, expand_ref[...]) for piece in _split3(d_last))[0:1]

    for g in range(SSM_GROUPS):
        gs = slice(g * n, (g + 1) * n)
        cbm_ref[g] = _dot_nt(cb_ref[:, gs], bb_ref[:, gs])
        bT_ref[g] = act_ref[:, b_off + g * n:b_off + (g + 1) * n].T

    left_q = lax.broadcasted_iota(jnp.int32, (q, lanes), 1) < SSM_HEAD_DIM
    left_n = lax.broadcasted_iota(jnp.int32, (n, lanes), 1) < SSM_HEAD_DIM

    def operands(hp):
        g = hp // pairs_per_group
        sl = slice(hp * lanes, (hp + 1) * lanes)
        cbm = cbm_ref[g]
        cg32 = act_ref[:, c_off + g * n:c_off + (g + 1) * n]
        bgT = bT_ref[g]
        tops, btws = [], []
        for h in (2 * hp, 2 * hp + 1):
            a_b = jnp.broadcast_to(acs[:, h:h + 1], (q, n))
            inter = jnp.exp2(a_b) * cg32
            intra = cbm * jnp.exp2(jnp.where(causal, a_b[:, :q] - row_term[h:h + 1, :], -jnp.inf))
            tops.append(jnp.concatenate([inter.astype(BF16), intra.astype(BF16)], axis=1))
            btws.append((bgT * w_rows[h:h + 1, :]).astype(BF16))
        rhs = jnp.concatenate([state_ref[:, sl].astype(BF16), xsb_ref[:, sl]], axis=0)
        return jnp.concatenate(tops, axis=0), jnp.concatenate(btws, axis=0), rhs

    def products(ops):
        lhs_y, lhs_s, rhs = ops
        return _dot(lhs_y, rhs), _dot(lhs_s, rhs[n:])

    def commit(hp, res):
        res_y, res_s = res
        sl = slice(hp * lanes, (hp + 1) * lanes)
        yacc_ref[:, sl] = jnp.where(left_q, res_y[:q], res_y[q:])
        state_ref[:, sl] = state_ref[:, sl] * d_lanes[:, sl] + jnp.where(left_n, res_s[:n], res_s[n:])

    ops, res = {}, {}
    for t in range(pairs + 2):
        if t < pairs:
            ops[t] = operands(t)
        if 0 <= t - 1 < pairs:
            res[t - 1] = products(ops.pop(t - 1))
        if 0 <= t - 2 < pairs:
            commit(t - 2, res.pop(t - 2))

    gw = d_inner // SSM_GROUPS
    for g in range(SSM_GROUPS):
        sl = slice(g * gw, (g + 1) * gw)
        zg = z_ref[0, :, sl]
        yg = (yacc_ref[:, sl] + act_ref[:, sl] * dskip_ref[:, sl]) * (zg * _sigmoid(zg))
        yg = yg * lax.rsqrt(jnp.mean(yg * yg, axis=-1, keepdims=True) + EPS)
        y_ref[0, :, sl] = (yg * gnw_ref[:, sl]).astype(y_ref.dtype)

    @pl.when(c == pl.num_programs(1) - 1)
    def _():
        hout_ref[0] = state_ref[...]


def _ssd(xbc, z, dt, dtT, h0, c0, convw, convb, dtb, dtbT, alog, alogT, dskip, gnw, expand, *, q):
    b, l, conv_dim = xbc.shape
    d_inner = z.shape[2]
    nh = dt.shape[1]
    nc = l // q
    gn = SSM_GROUPS * SSM_STATE
    lanes = 2 * SSM_HEAD_DIM
    kern = functools.partial(_ssd_kernel, q=q)
    return pl.pallas_call(
        kern,
        grid=(b, nc),
        in_specs=[pl.BlockSpec((1, q, conv_dim), lambda i, c: (i, c, 0)),
                  pl.BlockSpec((1, q, d_inner), lambda i, c: (i, c, 0)),
                  pl.BlockSpec((q, nh), lambda i, c: (i * nc + c, 0)),
                  pl.BlockSpec((1, nh, q), lambda i, c: (i * nc + c, 0, 0)),
                  pl.BlockSpec((1, SSM_STATE, d_inner), lambda i, c: (i, 0, 0)),
                  pl.BlockSpec((1, CARRY_ROWS, conv_dim), lambda i, c: (i, 0, 0)),
                  _resident(convw.shape), _resident(convb.shape),
                  _resident(dtb.shape), _resident(dtbT.shape),
                  _resident(alog.shape), _resident(alogT.shape),
                  _resident(dskip.shape), _resident(gnw.shape), _resident(expand.shape)],
        out_specs=[pl.BlockSpec((1, q, d_inner), lambda i, c: (i, c, 0)),
                   pl.BlockSpec((1, SSM_STATE, d_inner), lambda i, c: (i, 0, 0))],
        out_shape=[jax.ShapeDtypeStruct((b, l, d_inner), BF16),
                   jax.ShapeDtypeStruct((b, SSM_STATE, d_inner), F32)],
        scratch_shapes=[pltpu.VMEM((conv_dim // lanes, CARRY_ROWS + q, lanes), F32),
                        pltpu.VMEM((q, conv_dim), F32),
                        pltpu.VMEM((q, d_inner), BF16),
                        pltpu.VMEM((q, gn), BF16),
                        pltpu.VMEM((q, gn), BF16),
                        pltpu.VMEM((SSM_GROUPS, q, q), F32),
                        pltpu.VMEM((SSM_GROUPS, SSM_STATE, q), F32),
                        pltpu.VMEM((q, d_inner), F32),
                        pltpu.VMEM((SSM_STATE, d_inner), F32)],
        compiler_params=_cparams("parallel", "arbitrary"),
        name="ssd_mixer",
    )(xbc, z, dt, dtT, h0, c0, convw, convb, dtb, dtbT, alog, alogT, dskip, gnw, expand)


def _resproj_kernel(a_ref, w_ref, nw_ref, x_ref, o_ref):
    y = _dot(a_ref[...], w_ref[...])
    o_ref[...] = x_ref[...] + _rms(y, nw_ref[...])


def _resproj(a, w, nw, x, name):
    rows, k = a.shape
    d = w.shape[1]
    tm = ROW_TILE
    return pl.pallas_call(
        _resproj_kernel,
        grid=(rows // tm,),
        in_specs=[pl.BlockSpec((tm, k), lambda i: (i, 0)),
                  _resident(w.shape), _resident(nw.shape),
                  pl.BlockSpec((tm, d), lambda i: (i, 0))],
        out_specs=pl.BlockSpec((tm, d), lambda i: (i, 0)),
        out_shape=jax.ShapeDtypeStruct((rows, d), F32),
        compiler_params=_cparams("parallel"),
        name=name,
    )(a, w, nw, x)


def _ffn_kernel(x_ref, nw_in_ref, wg_ref, wu_ref, wo_ref, nw_out_ref, o_ref,
                xn_ref, h_ref, *, hc):
    xn_ref[...] = _rms(x_ref[...], nw_in_ref[...]).astype(BF16)
    hidden = wg_ref.shape[1]
    for c0 in range(0, hidden, hc):
        gate = _dot(xn_ref[...], wg_ref[:, c0:c0 + hc])
        up = _dot(xn_ref[...], wu_ref[:, c0:c0 + hc])
        h_ref[:, c0:c0 + hc] = (gate * _sigmoid(gate) * up).astype(BF16)
    y = _dot(h_ref[...], wo_ref[...])
    o_ref[...] = x_ref[...] + _rms(y, nw_out_ref[...])


def _ffn(x, nw_in, wg, wu, wo, nw_out, name):
    rows, d = x.shape
    hidden = wg.shape[1]
    tm = ROW_TILE
    kern = functools.partial(_ffn_kernel, hc=256)
    return pl.pallas_call(
        kern,
        grid=(rows // tm,),
        in_specs=[pl.BlockSpec((tm, d), lambda i: (i, 0)),
                  _resident(nw_in.shape), _resident(wg.shape), _resident(wu.shape),
                  _resident(wo.shape), _resident(nw_out.shape)],
        out_specs=pl.BlockSpec((tm, d), lambda i: (i, 0)),
        out_shape=jax.ShapeDtypeStruct((rows, d), F32),
        scratch_shapes=[pltpu.VMEM((tm, d), BF16), pltpu.VMEM((tm, hidden), BF16)],
        compiler_params=_cparams("parallel"),
        name=name,
    )(x, nw_in, wg, wu, wo, nw_out)


def _qkv_kernel(x_ref, nwq_ref, nwkv_ref, wq_ref, wk_ref, wv_ref,
                q_ref, k_ref, v_ref, kf_ref, vf_ref, *, pad_blocks, tail_blocks):
    i = pl.program_id(1)
    nb = pl.num_programs(1)

    @pl.when(i < pad_blocks)
    def _():
        k_ref[...] = jnp.zeros_like(k_ref)
        v_ref[...] = jnp.zeros_like(v_ref)

    @pl.when(i >= pad_blocks)
    def _():
        x = x_ref[0]
        xh = x * lax.rsqrt(jnp.mean(x * x, axis=-1, keepdims=True) + EPS)
        xq = (xh * nwq_ref[...]).astype(BF16)
        xkv = (xh * nwkv_ref[...]).astype(BF16)
        d = wq_ref.shape[1]
        for c0 in range(0, d, COL_TILE):
            sl = slice(c0, c0 + COL_TILE)
            q_ref[0, :, sl] = (_dot(xq, wq_ref[:, sl]) * Q_SCALE).astype(q_ref.dtype)
            kk = _dot(xkv, wk_ref[:, sl])
            vv = _dot(xkv, wv_ref[:, sl])
            k_ref[0, :, sl] = kk.astype(k_ref.dtype)
            v_ref[0, :, sl] = vv.astype(v_ref.dtype)

            @pl.when(i >= nb - tail_blocks)
            def _():
                kf_ref[0, :, sl] = kk
                vf_ref[0, :, sl] = vv


def _qkv(x, nwq, nwkv, wq, wk, wv, *, tm, pad_rows, tail_rows):
    b, l, d = x.shape
    pad_blocks = pad_rows // tm
    nblk = l // tm
    tail_blocks = tail_rows // tm
    kern = functools.partial(_qkv_kernel, pad_blocks=pad_blocks, tail_blocks=tail_blocks)
    src = lambda bi, i: (bi, jnp.maximum(i - pad_blocks, 0), 0)
    tail = lambda bi, i: (bi, jnp.maximum(i - pad_blocks - (nblk - tail_blocks), 0), 0)
    return pl.pallas_call(
        kern,
        grid=(b, nblk + pad_blocks),
        in_specs=[pl.BlockSpec((1, tm, d), src),
                  _resident(nwq.shape), _resident(nwkv.shape),
                  _resident(wq.shape), _resident(wk.shape), _resident(wv.shape)],
        out_specs=[pl.BlockSpec((1, tm, d), src),
                   pl.BlockSpec((1, tm, d), lambda bi, i: (bi, i, 0)),
                   pl.BlockSpec((1, tm, d), lambda bi, i: (bi, i, 0)),
                   pl.BlockSpec((1, tm, d), tail),
                   pl.BlockSpec((1, tm, d), tail)],
        out_shape=[jax.ShapeDtypeStruct((b, l, d), BF16),
                   jax.ShapeDtypeStruct((b, l + pad_rows, d), BF16),
                   jax.ShapeDtypeStruct((b, l + pad_rows, d), BF16),
                   jax.ShapeDtypeStruct((b, tail_rows, d), F32),
                   jax.ShapeDtypeStruct((b, tail_rows, d), F32)],
        compiler_params=_cparams("parallel", "arbitrary"),
        name="attn_qkv_proj",
    )(x, nwq, nwkv, wq, wk, wv)


def _bias_kernel(u_ref, o_ref):
    heads, width = u_ref.shape
    for h in range(heads):
        x = jnp.broadcast_to(u_ref[h:h + 1, :] * LOG2E, (CHUNK, width))
        r = pltpu.roll(x, width - CHUNK, 1, stride=1, stride_axis=0)
        o_ref[h] = r[:, :BAND]


def _bias_table(rel_bias):
    heads = rel_bias.shape[0]
    width = BAND + CHUNK
    far = jnp.broadcast_to(rel_bias[:, 2 * REL_CLIP:], (heads, width - (REL_CLIP + CHUNK)))
    near = jnp.flip(rel_bias[:, REL_CLIP - CHUNK + 1:], axis=1)
    u = jnp.concatenate([far, near], axis=1)
    return pl.pallas_call(
        _bias_kernel,
        out_shape=jax.ShapeDtypeStruct((heads, CHUNK, BAND), F32),
        name="attn_rel_bias_table",
    )(u)


def _attn_kernel(q_ref, k_ref, v_ref, bias_ref, o_ref, kbuf_ref, vbuf_ref, *, first_chunk):
    c = pl.program_id(1)

    @pl.when(c == 0)
    def _():
        kbuf_ref[...] = k_ref[0]
        vbuf_ref[...] = v_ref[0]

    start = pl.multiple_of(c * CHUNK, CHUNK)
    pairs = bias_ref.shape[0]
    pw = 2 * ATT_HEAD_DIM
    left = lax.broadcasted_iota(jnp.int32, (CHUNK, pw), 1) < ATT_HEAD_DIM

    def body(masked):
        if masked:
            j = lax.broadcasted_iota(jnp.int32, (2 * CHUNK, BAND), 1)
            valid = j >= (BAND_CHUNKS - (c + first_chunk)) * CHUNK

        def scores(hp):
            sl = slice(hp * pw, (hp + 1) * pw)
            qp = q_ref[0, :, sl]
            zero = jnp.zeros_like(qp)
            q2 = jnp.concatenate([jnp.where(left, qp, zero), jnp.where(left, zero, qp)], axis=0)
            kp = kbuf_ref[pl.ds(start, BAND), sl]
            s = _dot_nt(q2, kp) + bias_ref[hp]
            return jnp.where(valid, s, -jnp.inf) if masked else s

        def softmax(s):
            pr = jnp.exp2(s - jnp.max(s, axis=-1, keepdims=True))
            return pr.astype(BF16), jnp.sum(pr, axis=-1, keepdims=True)

        def weighted(hp, pr):
            return _dot(pr, vbuf_ref[pl.ds(start, BAND), hp * pw:(hp + 1) * pw])

        def finish(hp, r, denom):
            r = r / denom
            o_ref[0, :, hp * pw:(hp + 1) * pw] = jnp.where(left, r[:CHUNK], r[CHUNK:]).astype(o_ref.dtype)

        s, pr, den, acc = {}, {}, {}, {}
        for t in range(pairs + 3):
            if t < pairs:
                s[t] = scores(t)
            if 0 <= t - 2 < pairs:
                acc[t - 2] = weighted(t - 2, pr.pop(t - 2))
            if 0 <= t - 1 < pairs:
                pr[t - 1], den[t - 1] = softmax(s.pop(t - 1))
            if 0 <= t - 3 < pairs:
                finish(t - 3, acc.pop(t - 3), den.pop(t - 3))

    if first_chunk >= BAND_CHUNKS:
        body(False)
    else:
        band_full = c + first_chunk >= BAND_CHUNKS
        pl.when(band_full)(lambda: body(False))
        pl.when(jnp.logical_not(band_full))(lambda: body(True))


def _attention(q, k, v, bias, *, first_chunk):
    b, l, d = q.shape
    lk = k.shape[1]
    kern = functools.partial(_attn_kernel, first_chunk=first_chunk)
    return pl.pallas_call(
        kern,
        grid=(b, l // CHUNK),
        in_specs=[pl.BlockSpec((1, CHUNK, d), lambda i, c: (i, c, 0)),
                  pl.BlockSpec((1, lk, d), lambda i, c: (i, 0, 0)),
                  pl.BlockSpec((1, lk, d), lambda i, c: (i, 0, 0)),
                  _resident(bias.shape)],
        out_specs=pl.BlockSpec((1, CHUNK, d), lambda i, c: (i, c, 0)),
        out_shape=jax.ShapeDtypeStruct((b, l, d), BF16),
        scratch_shapes=[pltpu.VMEM((lk, d), BF16), pltpu.VMEM((lk, d), BF16)],
        compiler_params=_cparams("parallel", "arbitrary"),
        name="band_attention",
    )(q, k, v, bias)


def _row(v):
    return v.reshape(1, -1).astype(F32)


def _ssd_layer(x3, h0, conv_prev, p, nw, *, q):
    b, l, d = x3.shape
    rows = b * l
    d_inner = p["wz"].shape[1]
    conv_dim = p["wx"].shape[1]
    x = x3.reshape(rows, d)
    z, xbc, dt, dtT = _inproj(x, _row(nw[0]), p["wz"], p["wx"], p["wdt"], p["wdtT"])
    xbc3 = xbc.reshape(b, l, conv_dim)
    h0t = jnp.transpose(h0, (0, 3, 1, 2)).reshape(b, SSM_STATE, d_inner)
    c0 = jnp.pad(conv_prev, ((0, 0), (CARRY_ROWS - (CONV_W - 1), 0), (0, 0)))
    dtT = jnp.transpose(dtT.reshape(-1, rows // q, q), (1, 0, 2))
    y, hT = _ssd(xbc3, z.reshape(b, l, d_inner), dt, dtT, h0t, c0,
                 p["convw"], p["convb"], p["dtb"], p["dtbT"], p["alog"], p["alogT"],
                 p["dskip"], p["gnw"], p["expand"], q=q)
    h_new = jnp.transpose(hT.reshape(b, SSM_STATE, -1, SSM_HEAD_DIM), (0, 2, 3, 1))
    new_conv = xbc3[:, l - (CONV_W - 1):]
    x1 = _resproj(y.reshape(rows, d_inner), p["wout"], _row(nw[1]), x, "ssm_out_proj")
    return x1.reshape(b, l, d), h_new, new_conv


def _ffn_layer(x3, p, nw, name):
    b, l, d = x3.shape
    out = _ffn(x3.reshape(b * l, d), _row(nw[2]), p["wg"], p["wu"], p["wo"], _row(nw[3]), name)
    return out.reshape(b, l, d)


def kernel(x_prompt, x_sample, state_ssm, state_conv, cache_k, cache_v, norm_w,
           ssm_w_in, ssm_conv_w, ssm_conv_b, ssm_dt_bias, ssm_A_log, ssm_D, ssm_norm_w, ssm_w_out,
           kv_norm_w, w_kv, attn_w_q, attn_rel_bias, attn_w_o, ffn_w_in, ffn_w_out):
    bp, lp, d = x_prompt.shape
    bs, ls, _ = x_sample.shape
    n_a = ssm_w_in.shape[0]
    depth = norm_w.shape[0]
    d_inner = ssm_w_out.shape[1]
    heads = ssm_dt_bias.shape[1]
    conv_dim = ssm_conv_w.shape[2]
    att_dim = attn_w_q.shape[2]
    att_heads = att_dim // ATT_HEAD_DIM
    hidden = ffn_w_out.shape[1]

    xp, xs = x_prompt, x_sample
    ssm_p, conv_p, ssm_s, conv_s = [], [], [], []
    outs_kv = None
    for layer in range(depth):
        nw = norm_w[layer]
        fp = {"wg": ffn_w_in[layer][:, :hidden].astype(BF16),
              "wu": ffn_w_in[layer][:, hidden:].astype(BF16),
              "wo": ffn_w_out[layer].astype(BF16)}
        if layer < n_a:
            a = layer
            w_in = ssm_w_in[a]
            wdt = w_in[:, d_inner + conv_dim:]
            sp = {"wz": w_in[:, :d_inner].astype(BF16),
                  "wx": w_in[:, d_inner:d_inner + conv_dim].astype(BF16),
                  "wdt": wdt.astype(BF16), "wdtT": wdt.T.astype(BF16),
                  "convw": ssm_conv_w[a], "convb": _row(ssm_conv_b[a]),
                  "dtb": _row(ssm_dt_bias[a]), "dtbT": ssm_dt_bias[a].reshape(-1, 1),
                  "alog": _row(ssm_A_log[a]), "alogT": ssm_A_log[a].reshape(-1, 1),
                  "dskip": _row(jnp.repeat(ssm_D[a], SSM_HEAD_DIM)),
                  "gnw": _row(ssm_norm_w[a]), "wout": ssm_w_out[a].astype(BF16),
                  "expand": jnp.repeat(jnp.eye(heads, dtype=BF16), SSM_HEAD_DIM, axis=1)}
            h0 = jnp.zeros((bp, heads, SSM_HEAD_DIM, SSM_STATE), F32)
            c0 = jnp.zeros((bp, CONV_W - 1, conv_dim), F32)
            xp, hp_new, cp_new = _ssd_layer(xp, h0, c0, sp, nw, q=128)
            xs, hs_new, cs_new = _ssd_layer(xs, state_ssm[a], state_conv[a], sp, nw, q=ls)
            ssm_p.append(hp_new)
            conv_p.append(cp_new)
            ssm_s.append(hs_new)
            conv_s.append(cs_new)
        else:
            i = layer - n_a
            wq = attn_w_q[i].astype(BF16)
            wk = w_kv[:, :att_dim].astype(BF16)
            wv = w_kv[:, att_dim:].astype(BF16)
            wo = attn_w_o[i].astype(BF16)
            bias = _bias_table(attn_rel_bias[i]).reshape(att_heads // 2, 2 * CHUNK, BAND)
            rows_p = min(BAND_CHUNKS * CHUNK, lp)
            qp, kpb, vpb, kpf, vpf = _qkv(xp, _row(nw[0]), _row(kv_norm_w), wq, wk, wv,
                                          tm=ROW_TILE, pad_rows=BAND_CHUNKS * CHUNK,
                                          tail_rows=rows_p)
            ap = _attention(qp, kpb, vpb, bias, first_chunk=0)
            xp = _resproj(ap.reshape(bp * lp, att_dim), wo, _row(nw[1]),
                          xp.reshape(bp * lp, d), "attn_out_proj").reshape(bp, lp, d)
            qs, ksb, vsb, ksf, vsf = _qkv(xs, _row(nw[0]), _row(kv_norm_w), wq, wk, wv,
                                          tm=ls, pad_rows=0, tail_rows=ls)
            r = cache_k.shape[1]
            kband = jnp.concatenate([cache_k.reshape(bs, r, att_dim).astype(BF16), ksb], axis=1)
            vband = jnp.concatenate([cache_v.reshape(bs, r, att_dim).astype(BF16), vsb], axis=1)
            a_s = _attention(qs, kband, vband, bias, first_chunk=BAND_CHUNKS)
            xs = _resproj(a_s.reshape(bs * ls, att_dim), wo, _row(nw[1]),
                          xs.reshape(bs * ls, d), "attn_out_proj").reshape(bs, ls, d)
            if outs_kv is None:
                outs_kv = (kpf.reshape(bp, rows_p, att_heads, ATT_HEAD_DIM),
                           vpf.reshape(bp, rows_p, att_heads, ATT_HEAD_DIM),
                           ksf.reshape(bs, ls, att_heads, ATT_HEAD_DIM),
                           vsf.reshape(bs, ls, att_heads, ATT_HEAD_DIM))
        xp = _ffn_layer(xp, fp, nw, "ffn_prompt")
        xs = _ffn_layer(xs, fp, nw, "ffn_sample")
    kp_out, vp_out, ks_out, vs_out = outs_kv
    return (xp, xs, jnp.stack(ssm_p), jnp.stack(conv_p), kp_out, vp_out,
            jnp.stack(ssm_s), jnp.stack(conv_s), ks_out, vs_out)
```

```python
import functools

import jax
import jax.numpy as jnp
from jax import lax
from jax.experimental import pallas as pl
from jax.experimental.pallas import tpu as pltpu

F32 = jnp.float32
BF16 = jnp.bfloat16

EPS = 1e-6
CHUNK = 64
BAND_CHUNKS = 8
BAND = (BAND_CHUNKS + 1) * CHUNK
REL_CLIP = 256
SSM_HEAD_DIM = 64
SSM_GROUPS = 4
SSM_STATE = 128
CONV_W = 4
ATT_HEAD_DIM = 64
LOG2E = 1.4426950408889634
Q_SCALE = ATT_HEAD_DIM ** -0.5 * LOG2E
CARRY_ROWS = 8

VMEM_LIMIT_BYTES = 56 * 1024 * 1024
ROW_TILE = 512
COL_TILE = 512
ATTN_CHUNKS_PER_STEP = 4


def _cparams(*sem):
    return pltpu.CompilerParams(dimension_semantics=sem,
                                vmem_limit_bytes=VMEM_LIMIT_BYTES)


def _resident(shape):
    zeros = (0,) * len(shape)
    return pl.BlockSpec(shape, lambda *_: zeros, pipeline_mode=pl.Buffered(1))


def _rms(x, w):
    ms = jnp.mean(x * x, axis=-1, keepdims=True)
    return x * lax.rsqrt(ms + EPS) * w


def _sigmoid(x):
    return 1.0 / (1.0 + jnp.exp2(x * -LOG2E))


def _softplus(x):
    return jnp.maximum(x, 0.0) + jnp.log(1.0 + jnp.exp(-jnp.abs(x)))


def _dot(a, b):
    return jnp.dot(a, b, preferred_element_type=F32)


def _dot_nt(a, b):
    return lax.dot_general(a, b, (((1,), (1,)), ((), ())), preferred_element_type=F32)


def _split3(x):
    hi = x.astype(BF16)
    r1 = x - hi.astype(F32)
    mid = r1.astype(BF16)
    lo = (r1 - mid.astype(F32)).astype(BF16)
    return hi, mid, lo


def _inproj_kernel(x_ref, nw_ref, wz_ref, wx_ref, wdt_ref, wdtT_ref,
                   z_ref, xbc_ref, dt_ref, dtT_ref):
    xn = _rms(x_ref[...], nw_ref[...]).astype(BF16)
    for w_ref, o_ref in ((wz_ref, z_ref), (wx_ref, xbc_ref)):
        n = w_ref.shape[1]
        for c0 in range(0, n, COL_TILE):
            o_ref[:, c0:c0 + COL_TILE] = _dot(xn, w_ref[:, c0:c0 + COL_TILE]).astype(o_ref.dtype)
    dt_ref[...] = _dot(xn, wdt_ref[...])
    dtT_ref[...] = _dot_nt(wdtT_ref[...], xn)


def _inproj(x, nw, wz, wx, wdt, wdtT):
    rows, d = x.shape
    tm = ROW_TILE
    nh = wdt.shape[1]
    return pl.pallas_call(
        _inproj_kernel,
        grid=(rows // tm,),
        in_specs=[pl.BlockSpec((tm, d), lambda i: (i, 0)),
                  _resident(nw.shape), _resident(wz.shape), _resident(wx.shape),
                  _resident(wdt.shape), _resident(wdtT.shape)],
        out_specs=[pl.BlockSpec((tm, wz.shape[1]), lambda i: (i, 0)),
                   pl.BlockSpec((tm, wx.shape[1]), lambda i: (i, 0)),
                   pl.BlockSpec((tm, nh), lambda i: (i, 0)),
                   pl.BlockSpec((nh, tm), lambda i: (0, i))],
        out_shape=[jax.ShapeDtypeStruct((rows, wz.shape[1]), F32),
                   jax.ShapeDtypeStruct((rows, wx.shape[1]), F32),
                   jax.ShapeDtypeStruct((rows, nh), F32),
                   jax.ShapeDtypeStruct((nh, rows), F32)],
        compiler_params=_cparams("parallel"),
        name="ssm_in_proj",
    )(x, nw, wz, wx, wdt, wdtT)


def _ssd_kernel(*refs, q, zero_init):
    xbc_ref, z_ref, dt_ref, dtT_ref = refs[:4]
    h0_ref, c0_ref = (None, None) if zero_init else refs[4:6]
    (convw_ref, convb_ref, dtb_ref, dtbT_ref, alog_ref, alogT_ref, dskip_ref, gnw_ref, expand_ref,
     y_ref, hout_ref,
     win_ref, act_ref, xsb_ref, bb_ref, cb_ref, cbm_ref, bT_ref, yacc_ref,
     state_ref) = refs[4 if zero_init else 6:]
    c = pl.program_id(1)
    d_inner = y_ref.shape[2]
    n = SSM_STATE
    gn = SSM_GROUPS * n
    lanes = 2 * SSM_HEAD_DIM
    pairs = d_inner // lanes
    pairs_per_group = pairs // SSM_GROUPS
    panels = win_ref.shape[0]
    b_off, c_off = d_inner, d_inner + gn

    @pl.when(c == 0)
    def _():
        if zero_init:
            state_ref[...] = jnp.zeros_like(state_ref)
            win_ref[:, 0:CARRY_ROWS, :] = jnp.zeros((panels, CARRY_ROWS, lanes), F32)
        else:
            state_ref[...] = h0_ref[0]
            for pn in range(panels):
                win_ref[pn, 0:CARRY_ROWS, :] = c0_ref[0, :, pn * lanes:(pn + 1) * lanes]

    for pn in range(panels):
        sl = slice(pn * lanes, (pn + 1) * lanes)
        win_ref[pn, CARRY_ROWS:CARRY_ROWS + q, :] = xbc_ref[0, :, sl]
        conv = convb_ref[:, sl]
        for k in range(CONV_W):
            r0 = CARRY_ROWS - (CONV_W - 1) + k
            conv = conv + win_ref[pn, r0:r0 + q, :] * convw_ref[k:k + 1, sl]
        win_ref[pn, 0:CARRY_ROWS, :] = win_ref[pn, q:q + CARRY_ROWS, :]
        act = conv * _sigmoid(conv)
        act_ref[:, sl] = act
        if pn * lanes < b_off:
            xsb_ref[:, sl] = act.astype(BF16)
        elif pn * lanes < c_off:
            bb_ref[:, pn * lanes - b_off:(pn + 1) * lanes - b_off] = act.astype(BF16)
        else:
            cb_ref[:, pn * lanes - c_off:(pn + 1) * lanes - c_off] = act.astype(BF16)

    dt = _softplus(dt_ref[...] + dtb_ref[...])
    dtT = _softplus(dtT_ref[0] + dtbT_ref[...])
    dA = dt * (-LOG2E * jnp.exp(alog_ref[...]))
    dAT = dtT * (-LOG2E * jnp.exp(alogT_ref[...]))
    row = lax.broadcasted_iota(jnp.int32, (q, q), 0)
    col = lax.broadcasted_iota(jnp.int32, (q, q), 1)
    causal = row >= col
    tril = jnp.where(causal, 1.0, 0.0).astype(BF16)
    triu = jnp.where(row <= col, 1.0, 0.0).astype(BF16)
    acs = sum(_dot(tril, piece) for piece in _split3(dA))
    acsT = sum(_dot(piece, triu) for piece in _split3(dAT))
    a_lastT = acsT[:, q - 1:q]
    row_term = acsT - jnp.log(dtT) * LOG2E
    w_rows = dtT * jnp.exp2(a_lastT - acsT)
    d_last = jnp.broadcast_to(jnp.exp2(acs[q - 1:q, :]), (8, acs.shape[1]))
    d_lanes = sum(_dot(piece, expand_ref[...]) for piece in _split3(d_last))[0:1]

    for g in range(SSM_GROUPS):
        gs = slice(g * n, (g + 1) * n)
        cbm_ref[g] = _dot_nt(cb_ref[:, gs], bb_ref[:, gs])
        bT_ref[g] = act_ref[:, b_off + g * n:b_off + (g + 1) * n].T

    left_q = lax.broadcasted_iota(jnp.int32, (q, lanes), 1) < SSM_HEAD_DIM
    left_n = lax.broadcasted_iota(jnp.int32, (n, lanes), 1) < SSM_HEAD_DIM

    def operands(hp):
        g = hp // pairs_per_group
        sl = slice(hp * lanes, (hp + 1) * lanes)
        cbm = cbm_ref[g]
        cg32 = act_ref[:, c_off + g * n:c_off + (g + 1) * n]
        bgT = bT_ref[g]
        tops, btws = [], []
        for h in (2 * hp, 2 * hp + 1):
            a_b = jnp.broadcast_to(acs[:, h:h + 1], (q, n))
            inter = jnp.exp2(a_b) * cg32
            intra = cbm * jnp.exp2(jnp.where(causal, a_b[:, :q] - row_term[h:h + 1, :], -jnp.inf))
            tops.append(jnp.concatenate([inter.astype(BF16), intra.astype(BF16)], axis=1))
            btws.append((bgT * w_rows[h:h + 1, :]).astype(BF16))
        rhs = jnp.concatenate([state_ref[:, sl].astype(BF16), xsb_ref[:, sl]], axis=0)
        return jnp.concatenate(tops, axis=0), jnp.concatenate(btws, axis=0), rhs

    def products(ops):
        lhs_y, lhs_s, rhs = ops
        return _dot(lhs_y, rhs), _dot(lhs_s, rhs[n:])

    def commit(hp, res):
        res_y, res_s = res
        sl = slice(hp * lanes, (hp + 1) * lanes)
        yacc_ref[:, sl] = jnp.where(left_q, res_y[:q], res_y[q:])
        state_ref[:, sl] = state_ref[:, sl] * d_lanes[:, sl] + jnp.where(left_n, res_s[:n], res_s[n:])

    ops, res = {}, {}
    for t in range(pairs + 2):
        if t < pairs:
            ops[t] = operands(t)
        if 0 <= t - 1 < pairs:
            res[t - 1] = products(ops.pop(t - 1))
        if 0 <= t - 2 < pairs:
            commit(t - 2, res.pop(t - 2))

    gw = d_inner // SSM_GROUPS
    for g in range(SSM_GROUPS):
        sl = slice(g * gw, (g + 1) * gw)
        zg = z_ref[0, :, sl]
        yg = (yacc_ref[:, sl] + act_ref[:, sl] * dskip_ref[:, sl]) * (zg * _sigmoid(zg))
        yg = yg * lax.rsqrt(jnp.mean(yg * yg, axis=-1, keepdims=True) + EPS)
        y_ref[0, :, sl] = (yg * gnw_ref[:, sl]).astype(y_ref.dtype)

    @pl.when(c == pl.num_programs(1) - 1)
    def _():
        hout_ref[0] = state_ref[...]


def _ssd(xbc, z, dt, dtT, state, convw, convb, dtb, dtbT, alog, alogT, dskip, gnw, expand, *, q):
    b, l, conv_dim = xbc.shape
    d_inner = z.shape[2]
    nh = dt.shape[1]
    nc = l // q
    gn = SSM_GROUPS * SSM_STATE
    lanes = 2 * SSM_HEAD_DIM
    kern = functools.partial(_ssd_kernel, q=q, zero_init=state is None)
    state_specs = [] if state is None else [
        pl.BlockSpec((1, SSM_STATE, d_inner), lambda i, c: (i, 0, 0)),
        pl.BlockSpec((1, CARRY_ROWS, conv_dim), lambda i, c: (i, 0, 0))]
    return pl.pallas_call(
        kern,
        grid=(b, nc),
        in_specs=[pl.BlockSpec((1, q, conv_dim), lambda i, c: (i, c, 0)),
                  pl.BlockSpec((1, q, d_inner), lambda i, c: (i, c, 0)),
                  pl.BlockSpec((q, nh), lambda i, c: (i * nc + c, 0)),
                  pl.BlockSpec((1, nh, q), lambda i, c: (i * nc + c, 0, 0)),
                  *state_specs,
                  _resident(convw.shape), _resident(convb.shape),
                  _resident(dtb.shape), _resident(dtbT.shape),
                  _resident(alog.shape), _resident(alogT.shape),
                  _resident(dskip.shape), _resident(gnw.shape), _resident(expand.shape)],
        out_specs=[pl.BlockSpec((1, q, d_inner), lambda i, c: (i, c, 0)),
                   pl.BlockSpec((1, SSM_STATE, d_inner), lambda i, c: (i, 0, 0))],
        out_shape=[jax.ShapeDtypeStruct((b, l, d_inner), BF16),
                   jax.ShapeDtypeStruct((b, SSM_STATE, d_inner), F32)],
        scratch_shapes=[pltpu.VMEM((conv_dim // lanes, CARRY_ROWS + q, lanes), F32),
                        pltpu.VMEM((q, conv_dim), F32),
                        pltpu.VMEM((q, d_inner), BF16),
                        pltpu.VMEM((q, gn), BF16),
                        pltpu.VMEM((q, gn), BF16),
                        pltpu.VMEM((SSM_GROUPS, q, q), F32),
                        pltpu.VMEM((SSM_GROUPS, SSM_STATE, q), F32),
                        pltpu.VMEM((q, d_inner), F32),
                        pltpu.VMEM((SSM_STATE, d_inner), F32)],
        compiler_params=_cparams("parallel", "arbitrary"),
        name="ssd_mixer",
    )(xbc, z, dt, dtT, *(state or ()), convw, convb, dtb, dtbT, alog, alogT, dskip, gnw, expand)


def _mix_ffn_kernel(a_ref, wmix_ref, nw_mix_ref, x_ref, nw_in_ref, wg_ref, wu_ref, wo_ref,
                    nw_out_ref, o_ref, xn_ref, h_ref, *, hc):
    x1 = x_ref[...] + _rms(_dot(a_ref[...], wmix_ref[...]), nw_mix_ref[...])
    o_ref[...] = x1
    xn_ref[...] = _rms(x1, nw_in_ref[...]).astype(BF16)
    hidden = wg_ref.shape[1]
    for c0 in range(0, hidden, hc):
        gate = _dot(xn_ref[...], wg_ref[:, c0:c0 + hc])
        up = _dot(xn_ref[...], wu_ref[:, c0:c0 + hc])
        h_ref[:, c0:c0 + hc] = (gate * _sigmoid(gate) * up).astype(BF16)
    y = _dot(h_ref[...], wo_ref[...])
    o_ref[...] = o_ref[...] + _rms(y, nw_out_ref[...])


def _mix_ffn(a, wmix, nw_mix, x, nw_in, wg, wu, wo, nw_out, name):
    rows, d = x.shape
    k = a.shape[1]
    hidden = wg.shape[1]
    tm = ROW_TILE
    kern = functools.partial(_mix_ffn_kernel, hc=256)
    return pl.pallas_call(
        kern,
        grid=(rows // tm,),
        in_specs=[pl.BlockSpec((tm, k), lambda i: (i, 0)),
                  _resident(wmix.shape), _resident(nw_mix.shape),
                  pl.BlockSpec((tm, d), lambda i: (i, 0)),
                  _resident(nw_in.shape), _resident(wg.shape), _resident(wu.shape),
                  _resident(wo.shape), _resident(nw_out.shape)],
        out_specs=pl.BlockSpec((tm, d), lambda i: (i, 0)),
        out_shape=jax.ShapeDtypeStruct((rows, d), F32),
        scratch_shapes=[pltpu.VMEM((tm, d), BF16), pltpu.VMEM((tm, hidden), BF16)],
        compiler_params=_cparams("parallel"),
        name=name,
    )(a, wmix, nw_mix, x, nw_in, wg, wu, wo, nw_out)


def _qkv_kernel(x_ref, nwq_ref, nwkv_ref, wq_ref, wk_ref, wv_ref,
                q_ref, k_ref, v_ref, kf_ref, vf_ref, *, pad_blocks, tail_blocks):
    i = pl.program_id(1)
    nb = pl.num_programs(1)

    @pl.when(i < pad_blocks)
    def _():
        k_ref[...] = jnp.zeros_like(k_ref)
        v_ref[...] = jnp.zeros_like(v_ref)

    @pl.when(i >= pad_blocks)
    def _():
        x = x_ref[0]
        xh = x * lax.rsqrt(jnp.mean(x * x, axis=-1, keepdims=True) + EPS)
        xq = (xh * nwq_ref[...]).astype(BF16)
        xkv = (xh * nwkv_ref[...]).astype(BF16)
        d = wq_ref.shape[1]
        for c0 in range(0, d, COL_TILE):
            sl = slice(c0, c0 + COL_TILE)
            q_ref[0, :, sl] = (_dot(xq, wq_ref[:, sl]) * Q_SCALE).astype(q_ref.dtype)
            kk = _dot(xkv, wk_ref[:, sl])
            vv = _dot(xkv, wv_ref[:, sl])
            k_ref[0, :, sl] = kk.astype(k_ref.dtype)
            v_ref[0, :, sl] = vv.astype(v_ref.dtype)

            @pl.when(i >= nb - tail_blocks)
            def _():
                kf_ref[0, :, sl] = kk
                vf_ref[0, :, sl] = vv


def _qkv(x, nwq, nwkv, wq, wk, wv, *, tm, pad_rows, tail_rows):
    b, l, d = x.shape
    pad_blocks = pad_rows // tm
    nblk = l // tm
    tail_blocks = tail_rows // tm
    kern = functools.partial(_qkv_kernel, pad_blocks=pad_blocks, tail_blocks=tail_blocks)
    src = lambda bi, i: (bi, jnp.maximum(i - pad_blocks, 0), 0)
    tail = lambda bi, i: (bi, jnp.maximum(i - pad_blocks - (nblk - tail_blocks), 0), 0)
    return pl.pallas_call(
        kern,
        grid=(b, nblk + pad_blocks),
        in_specs=[pl.BlockSpec((1, tm, d), src),
                  _resident(nwq.shape), _resident(nwkv.shape),
                  _resident(wq.shape), _resident(wk.shape), _resident(wv.shape)],
        out_specs=[pl.BlockSpec((1, tm, d), src),
                   pl.BlockSpec((1, tm, d), lambda bi, i: (bi, i, 0)),
                   pl.BlockSpec((1, tm, d), lambda bi, i: (bi, i, 0)),
                   pl.BlockSpec((1, tm, d), tail),
                   pl.BlockSpec((1, tm, d), tail)],
        out_shape=[jax.ShapeDtypeStruct((b, l, d), BF16),
                   jax.ShapeDtypeStruct((b, l + pad_rows, d), BF16),
                   jax.ShapeDtypeStruct((b, l + pad_rows, d), BF16),
                   jax.ShapeDtypeStruct((b, tail_rows, d), F32),
                   jax.ShapeDtypeStruct((b, tail_rows, d), F32)],
        compiler_params=_cparams("parallel", "arbitrary"),
        name="attn_qkv_proj",
    )(x, nwq, nwkv, wq, wk, wv)


def _bias_kernel(u_ref, o_ref):
    heads, width = u_ref.shape
    for h in range(heads):
        x = jnp.broadcast_to(u_ref[h:h + 1, :] * LOG2E, (CHUNK, width))
        r = pltpu.roll(x, width - CHUNK, 1, stride=1, stride_axis=0)
        o_ref[h] = r[:, :BAND]


def _bias_table(rel_bias):
    heads = rel_bias.shape[0]
    width = BAND + CHUNK
    far = jnp.broadcast_to(rel_bias[:, 2 * REL_CLIP:], (heads, width - (REL_CLIP + CHUNK)))
    near = jnp.flip(rel_bias[:, REL_CLIP - CHUNK + 1:], axis=1)
    u = jnp.concatenate([far, near], axis=1)
    return pl.pallas_call(
        _bias_kernel,
        out_shape=jax.ShapeDtypeStruct((heads, CHUNK, BAND), F32),
        name="attn_rel_bias_table",
    )(u)


def _attn_kernel(q_ref, k_ref, v_ref, bias_ref, o_ref, kbuf_ref, vbuf_ref, *, first_chunk, cps):
    step = pl.program_id(1)

    @pl.when(step == 0)
    def _():
        kbuf_ref[...] = k_ref[0]
        vbuf_ref[...] = v_ref[0]

    pairs = bias_ref.shape[0]
    pw = 2 * ATT_HEAD_DIM
    left = lax.broadcasted_iota(jnp.int32, (CHUNK, pw), 1) < ATT_HEAD_DIM
    units = [(ci, hp) for ci in range(cps) for hp in range(pairs)]

    def body(masked):
        valid = {}
        if masked:
            j = lax.broadcasted_iota(jnp.int32, (2 * CHUNK, BAND), 1)
            for ci in range(cps):
                valid[ci] = j >= (BAND_CHUNKS - (step * cps + ci + first_chunk)) * CHUNK

        def band(ref, u):
            ci, hp = units[u]
            start = pl.multiple_of((step * cps + ci) * CHUNK, CHUNK)
            return ref[pl.ds(start, BAND), hp * pw:(hp + 1) * pw]

        def scores(u):
            ci, hp = units[u]
            qp = q_ref[0, ci * CHUNK:(ci + 1) * CHUNK, hp * pw:(hp + 1) * pw]
            zero = jnp.zeros_like(qp)
            q2 = jnp.concatenate([jnp.where(left, qp, zero), jnp.where(left, zero, qp)], axis=0)
            s = _dot_nt(q2, band(kbuf_ref, u)) + bias_ref[hp]
            return jnp.where(valid[ci], s, -jnp.inf) if masked else s

        def softmax(s):
            pr = jnp.exp2(s - jnp.max(s, axis=-1, keepdims=True))
            return pr.astype(BF16), jnp.sum(pr, axis=-1, keepdims=True)

        def weighted(u, pr):
            v = band(vbuf_ref, u)
            past = BAND - CHUNK
            return _dot(pr[:, :past], v[:past]) + _dot(pr[:, past:], v[past:])

        def finish(u, r, denom):
            ci, hp = units[u]
            r = r / denom
            o_ref[0, ci * CHUNK:(ci + 1) * CHUNK, hp * pw:(hp + 1) * pw] = (
                jnp.where(left, r[:CHUNK], r[CHUNK:]).astype(o_ref.dtype))

        n = len(units)
        s, pr, den, acc = {}, {}, {}, {}
        for t in range(n + 3):
            if t < n:
                s[t] = scores(t)
            if 0 <= t - 2 < n:
                acc[t - 2] = weighted(t - 2, pr.pop(t - 2))
            if 0 <= t - 1 < n:
                pr[t - 1], den[t - 1] = softmax(s.pop(t - 1))
            if 0 <= t - 3 < n:
                finish(t - 3, acc.pop(t - 3), den.pop(t - 3))

    if first_chunk >= BAND_CHUNKS:
        body(False)
    else:
        band_full = step * cps + first_chunk >= BAND_CHUNKS
        pl.when(band_full)(lambda: body(False))
        pl.when(jnp.logical_not(band_full))(lambda: body(True))


def _attention(q, k, v, bias, *, first_chunk, cps):
    b, l, d = q.shape
    lk = k.shape[1]
    assert BAND_CHUNKS % cps == 0 and l % (cps * CHUNK) == 0
    kern = functools.partial(_attn_kernel, first_chunk=first_chunk, cps=cps)
    return pl.pallas_call(
        kern,
        grid=(b, l // (cps * CHUNK)),
        in_specs=[pl.BlockSpec((1, cps * CHUNK, d), lambda i, c: (i, c, 0)),
                  pl.BlockSpec((1, lk, d), lambda i, c: (i, 0, 0)),
                  pl.BlockSpec((1, lk, d), lambda i, c: (i, 0, 0)),
                  _resident(bias.shape)],
        out_specs=pl.BlockSpec((1, cps * CHUNK, d), lambda i, c: (i, c, 0)),
        out_shape=jax.ShapeDtypeStruct((b, l, d), BF16),
        scratch_shapes=[pltpu.VMEM((lk, d), BF16), pltpu.VMEM((lk, d), BF16)],
        compiler_params=_cparams("parallel", "arbitrary"),
        name="band_attention",
    )(q, k, v, bias)


def _row(v):
    return v.reshape(1, -1).astype(F32)


def _ssd_layer(x3, state, p, nw, *, q):
    b, l, d = x3.shape
    rows = b * l
    d_inner = p["wz"].shape[1]
    conv_dim = p["wx"].shape[1]
    x = x3.reshape(rows, d)
    z, xbc, dt, dtT = _inproj(x, _row(nw[0]), p["wz"], p["wx"], p["wdt"], p["wdtT"])
    xbc3 = xbc.reshape(b, l, conv_dim)
    if state is not None:
        h0, conv_prev = state
        state = (jnp.transpose(h0, (0, 3, 1, 2)).reshape(b, SSM_STATE, d_inner),
                 jnp.pad(conv_prev, ((0, 0), (CARRY_ROWS - (CONV_W - 1), 0), (0, 0))))
    dtT = jnp.transpose(dtT.reshape(-1, rows // q, q), (1, 0, 2))
    y, hT = _ssd(xbc3, z.reshape(b, l, d_inner), dt, dtT, state,
                 p["convw"], p["convb"], p["dtb"], p["dtbT"], p["alog"], p["alogT"],
                 p["dskip"], p["gnw"], p["expand"], q=q)
    h_new = jnp.transpose(hT.reshape(b, SSM_STATE, -1, SSM_HEAD_DIM), (0, 2, 3, 1))
    new_conv = xbc3[:, l - (CONV_W - 1):]
    return y.reshape(rows, d_inner), h_new, new_conv


def _layer_tail(mixed, w_mix, x3, fp, nw, name):
    b, l, d = x3.shape
    out = _mix_ffn(mixed, w_mix, _row(nw[1]), x3.reshape(b * l, d), _row(nw[2]),
                   fp["wg"], fp["wu"], fp["wo"], _row(nw[3]), name)
    return out.reshape(b, l, d)


def kernel(x_prompt, x_sample, state_ssm, state_conv, cache_k, cache_v, norm_w,
           ssm_w_in, ssm_conv_w, ssm_conv_b, ssm_dt_bias, ssm_A_log, ssm_D, ssm_norm_w, ssm_w_out,
           kv_norm_w, w_kv, attn_w_q, attn_rel_bias, attn_w_o, ffn_w_in, ffn_w_out):
    bp, lp, d = x_prompt.shape
    bs, ls, _ = x_sample.shape
    n_a = ssm_w_in.shape[0]
    depth = norm_w.shape[0]
    d_inner = ssm_w_out.shape[1]
    heads = ssm_dt_bias.shape[1]
    conv_dim = ssm_conv_w.shape[2]
    att_dim = attn_w_q.shape[2]
    att_heads = att_dim // ATT_HEAD_DIM
    hidden = ffn_w_out.shape[1]

    xp, xs = x_prompt, x_sample
    ssm_p, conv_p, ssm_s, conv_s = [], [], [], []
    outs_kv = None
    for layer in range(depth):
        nw = norm_w[layer]
        fp = {"wg": ffn_w_in[layer][:, :hidden].astype(BF16),
              "wu": ffn_w_in[layer][:, hidden:].astype(BF16),
              "wo": ffn_w_out[layer].astype(BF16)}
        if layer < n_a:
            a = layer
            w_in = ssm_w_in[a]
            wdt = w_in[:, d_inner + conv_dim:]
            sp = {"wz": w_in[:, :d_inner].astype(BF16),
                  "wx": w_in[:, d_inner:d_inner + conv_dim].astype(BF16),
                  "wdt": wdt.astype(BF16), "wdtT": wdt.T.astype(BF16),
                  "convw": ssm_conv_w[a], "convb": _row(ssm_conv_b[a]),
                  "dtb": _row(ssm_dt_bias[a]), "dtbT": ssm_dt_bias[a].reshape(-1, 1),
                  "alog": _row(ssm_A_log[a]), "alogT": ssm_A_log[a].reshape(-1, 1),
                  "dskip": _row(jnp.repeat(ssm_D[a], SSM_HEAD_DIM)),
                  "gnw": _row(ssm_norm_w[a]), "wout": ssm_w_out[a].astype(BF16),
                  "expand": jnp.repeat(jnp.eye(heads, dtype=BF16), SSM_HEAD_DIM, axis=1)}
            mp, hp_new, cp_new = _ssd_layer(xp, None, sp, nw, q=128)
            ms, hs_new, cs_new = _ssd_layer(xs, (state_ssm[a], state_conv[a]), sp, nw, q=ls)
            w_mix = sp["wout"]
            ssm_p.append(hp_new)
            conv_p.append(cp_new)
            ssm_s.append(hs_new)
            conv_s.append(cs_new)
        else:
            i = layer - n_a
            wq = attn_w_q[i].astype(BF16)
            wk = w_kv[:, :att_dim].astype(BF16)
            wv = w_kv[:, att_dim:].astype(BF16)
            w_mix = attn_w_o[i].astype(BF16)
            bias = _bias_table(attn_rel_bias[i]).reshape(att_heads // 2, 2 * CHUNK, BAND)
            rows_p = min(BAND_CHUNKS * CHUNK, lp)
            qp, kpb, vpb, kpf, vpf = _qkv(xp, _row(nw[0]), _row(kv_norm_w), wq, wk, wv,
                                          tm=ROW_TILE, pad_rows=BAND_CHUNKS * CHUNK,
                                          tail_rows=rows_p)
            mp = _attention(qp, kpb, vpb, bias, first_chunk=0,
                            cps=ATTN_CHUNKS_PER_STEP).reshape(bp * lp, att_dim)
            qs, ksb, vsb, ksf, vsf = _qkv(xs, _row(nw[0]), _row(kv_norm_w), wq, wk, wv,
                                          tm=ls, pad_rows=0, tail_rows=ls)
            r = cache_k.shape[1]
            kband = jnp.concatenate([cache_k.reshape(bs, r, att_dim).astype(BF16), ksb], axis=1)
            vband = jnp.concatenate([cache_v.reshape(bs, r, att_dim).astype(BF16), vsb], axis=1)
            ms = _attention(qs, kband, vband, bias, first_chunk=BAND_CHUNKS,
                            cps=1).reshape(bs * ls, att_dim)
            if outs_kv is None:
                outs_kv = (kpf.reshape(bp, rows_p, att_heads, ATT_HEAD_DIM),
                           vpf.reshape(bp, rows_p, att_heads, ATT_HEAD_DIM),
                           ksf.reshape(bs, ls, att_heads, ATT_HEAD_DIM),
                           vsf.reshape(bs, ls, att_heads, ATT_HEAD_DIM))
        xp = _layer_tail(mp, w_mix, xp, fp, nw, "mix_ffn_prompt")
        xs = _layer_tail(ms, w_mix, xs, fp, nw, "mix_ffn_sample")
    kp_out, vp_out, ks_out, vs_out = outs_kv
    return (xp, xs, jnp.stack(ssm_p), jnp.stack(conv_p), kp_out, vp_out,
            jnp.stack(ssm_s), jnp.stack(conv_s), ks_out, vs_out)
```

```python
import functools

import jax
import jax.numpy as jnp
from jax import lax
from jax.experimental import pallas as pl
from jax.experimental.pallas import tpu as pltpu

F32 = jnp.float32
BF16 = jnp.bfloat16

EPS = 1e-6
CHUNK = 64
BAND_CHUNKS = 8
BAND = (BAND_CHUNKS + 1) * CHUNK
REL_CLIP = 256
SSM_HEAD_DIM = 64
SSM_GROUPS = 4
SSM_STATE = 128
CONV_W = 4
ATT_HEAD_DIM = 64
LOG2E = 1.4426950408889634
Q_SCALE = ATT_HEAD_DIM ** -0.5 * LOG2E
CARRY_ROWS = 8

VMEM_LIMIT_BYTES = 56 * 1024 * 1024
ROW_TILE = 512
COL_TILE = 512
ATTN_CHUNKS_PER_STEP = 4


def _cparams(*sem):
    return pltpu.CompilerParams(dimension_semantics=sem,
                                vmem_limit_bytes=VMEM_LIMIT_BYTES)


def _resident(shape):
    zeros = (0,) * len(shape)
    return pl.BlockSpec(shape, lambda *_: zeros, pipeline_mode=pl.Buffered(1))


def _rms(x, w):
    ms = jnp.mean(x * x, axis=-1, keepdims=True)
    return x * lax.rsqrt(ms + EPS) * w


def _sigmoid(x):
    return 1.0 / (1.0 + jnp.exp2(x * -LOG2E))


def _softplus(x):
    return jnp.maximum(x, 0.0) + jnp.log(1.0 + jnp.exp(-jnp.abs(x)))


def _dot(a, b):
    return jnp.dot(a, b, preferred_element_type=F32)


def _dot_nt(a, b):
    return lax.dot_general(a, b, (((1,), (1,)), ((), ())), preferred_element_type=F32)


def _split3(x):
    hi = x.astype(BF16)
    r1 = x - hi.astype(F32)
    mid = r1.astype(BF16)
    lo = (r1 - mid.astype(F32)).astype(BF16)
    return hi, mid, lo


def _inproj_kernel(x_ref, nw_ref, wz_ref, wx_ref, wdt_ref,
                   z_ref, xbc_ref, dt_ref, dtT_ref):
    xn = _rms(x_ref[...], nw_ref[...]).astype(BF16)
    for w_ref, o_ref in ((wz_ref, z_ref), (wx_ref, xbc_ref)):
        n = w_ref.shape[1]
        for c0 in range(0, n, COL_TILE):
            o_ref[:, c0:c0 + COL_TILE] = _dot(xn, w_ref[:, c0:c0 + COL_TILE]).astype(o_ref.dtype)
    nh = dt_ref.shape[1]
    dt = _dot(xn, wdt_ref[...])
    dt_ref[...] = dt[:, :nh]
    dtT_ref[...] = dt.T[:nh, :]


def _inproj(x, nw, wz, wx, wdt, nh):
    rows, d = x.shape
    tm = ROW_TILE
    return pl.pallas_call(
        _inproj_kernel,
        grid=(rows // tm,),
        in_specs=[pl.BlockSpec((tm, d), lambda i: (i, 0)),
                  _resident(nw.shape), _resident(wz.shape), _resident(wx.shape),
                  _resident(wdt.shape)],
        out_specs=[pl.BlockSpec((tm, wz.shape[1]), lambda i: (i, 0)),
                   pl.BlockSpec((tm, wx.shape[1]), lambda i: (i, 0)),
                   pl.BlockSpec((tm, nh), lambda i: (i, 0)),
                   pl.BlockSpec((nh, tm), lambda i: (0, i))],
        out_shape=[jax.ShapeDtypeStruct((rows, wz.shape[1]), F32),
                   jax.ShapeDtypeStruct((rows, wx.shape[1]), F32),
                   jax.ShapeDtypeStruct((rows, nh), F32),
                   jax.ShapeDtypeStruct((nh, rows), F32)],
        compiler_params=_cparams("parallel"),
        name="ssm_in_proj",
    )(x, nw, wz, wx, wdt)


def _ssd_kernel(*refs, q, zero_init):
    xbc_ref, z_ref, dt_ref, dtT_ref = refs[:4]
    h0_ref, c0_ref = (None, None) if zero_init else refs[4:6]
    (convw_ref, convb_ref, dtb_ref, dtbT_ref, alog_ref, alogT_ref, dskip_ref, gnw_ref, expand_ref,
     y_ref, hout_ref,
     win_ref, act_ref, xsb_ref, bb_ref, cb_ref, cbm_ref, bT_ref, yacc_ref,
     state_ref) = refs[4 if zero_init else 6:]
    c = pl.program_id(1)
    d_inner = y_ref.shape[2]
    n = SSM_STATE
    gn = SSM_GROUPS * n
    lanes = 2 * SSM_HEAD_DIM
    pairs = d_inner // lanes
    pairs_per_group = pairs // SSM_GROUPS
    panels = win_ref.shape[0]
    b_off, c_off = d_inner, d_inner + gn

    @pl.when(c == 0)
    def _():
        if zero_init:
            state_ref[...] = jnp.zeros_like(state_ref)
            win_ref[:, 0:CARRY_ROWS, :] = jnp.zeros((panels, CARRY_ROWS, lanes), F32)
        else:
            state_ref[...] = h0_ref[0]
            for pn in range(panels):
                win_ref[pn, 0:CARRY_ROWS, :] = c0_ref[0, :, pn * lanes:(pn + 1) * lanes]

    for pn in range(panels):
        sl = slice(pn * lanes, (pn + 1) * lanes)
        win_ref[pn, CARRY_ROWS:CARRY_ROWS + q, :] = xbc_ref[0, :, sl]
        conv = convb_ref[:, sl]
        for k in range(CONV_W):
            r0 = CARRY_ROWS - (CONV_W - 1) + k
            conv = conv + win_ref[pn, r0:r0 + q, :] * convw_ref[k:k + 1, sl]
        win_ref[pn, 0:CARRY_ROWS, :] = win_ref[pn, q:q + CARRY_ROWS, :]
        act = conv * _sigmoid(conv)
        act_ref[:, sl] = act
        if pn * lanes < b_off:
            xsb_ref[:, sl] = act.astype(BF16)
        elif pn * lanes < c_off:
            bb_ref[:, pn * lanes - b_off:(pn + 1) * lanes - b_off] = act.astype(BF16)
        else:
            cb_ref[:, pn * lanes - c_off:(pn + 1) * lanes - c_off] = act.astype(BF16)

    dt = _softplus(dt_ref[...] + dtb_ref[...])
    dtT = _softplus(dtT_ref[0] + dtbT_ref[...])
    dA = dt * (-LOG2E * jnp.exp(alog_ref[...]))
    dAT = dtT * (-LOG2E * jnp.exp(alogT_ref[...]))
    row = lax.broadcasted_iota(jnp.int32, (q, q), 0)
    col = lax.broadcasted_iota(jnp.int32, (q, q), 1)
    causal = row >= col
    tril = jnp.where(causal, 1.0, 0.0).astype(BF16)
    triu = jnp.where(row <= col, 1.0, 0.0).astype(BF16)
    acs = sum(_dot(tril, piece) for piece in _split3(dA))
    acsT = sum(_dot(piece, triu) for piece in _split3(dAT))
    a_lastT = acsT[:, q - 1:q]
    row_term = acsT - jnp.log(dtT) * LOG2E
    w_rows = dtT * jnp.exp2(a_lastT - acsT)
    d_last = jnp.broadcast_to(jnp.exp2(acs[q - 1:q, :]), (8, acs.shape[1]))
    d_lanes = sum(_dot(piece, expand_ref[...]) for piece in _split3(d_last))[0:1]

    for g in range(SSM_GROUPS):
        gs = slice(g * n, (g + 1) * n)
        cbm_ref[g] = _dot_nt(cb_ref[:, gs], bb_ref[:, gs])
        bT_ref[g] = act_ref[:, b_off + g * n:b_off + (g + 1) * n].T

    left_q = lax.broadcasted_iota(jnp.int32, (q, lanes), 1) < SSM_HEAD_DIM
    left_n = lax.broadcasted_iota(jnp.int32, (n, lanes), 1) < SSM_HEAD_DIM

    def operands(hp):
        g = hp // pairs_per_group
        sl = slice(hp * lanes, (hp + 1) * lanes)
        cbm = cbm_ref[g]
        cg32 = act_ref[:, c_off + g * n:c_off + (g + 1) * n]
        bgT = bT_ref[g]
        tops, btws = [], []
        for h in (2 * hp, 2 * hp + 1):
            a_b = jnp.broadcast_to(acs[:, h:h + 1], (q, n))
            inter = jnp.exp2(a_b) * cg32
            intra = cbm * jnp.exp2(jnp.where(causal, a_b[:, :q] - row_term[h:h + 1, :], -jnp.inf))
            tops.append(jnp.concatenate([inter.astype(BF16), intra.astype(BF16)], axis=1))
            btws.append((bgT * w_rows[h:h + 1, :]).astype(BF16))
        rhs = jnp.concatenate([state_ref[:, sl].astype(BF16), xsb_ref[:, sl]], axis=0)
        return jnp.concatenate(tops, axis=0), jnp.concatenate(btws, axis=0), rhs

    def products(ops):
        lhs_y, lhs_s, rhs = ops
        return _dot(lhs_y, rhs), _dot(lhs_s, rhs[n:])

    def commit(hp, res):
        res_y, res_s = res
        sl = slice(hp * lanes, (hp + 1) * lanes)
        yacc_ref[:, sl] = jnp.where(left_q, res_y[:q], res_y[q:])
        state_ref[:, sl] = state_ref[:, sl] * d_lanes[:, sl] + jnp.where(left_n, res_s[:n], res_s[n:])

    ops, res = {}, {}
    for t in range(pairs + 2):
        if t < pairs:
            ops[t] = operands(t)
        if 0 <= t - 1 < pairs:
            res[t - 1] = products(ops.pop(t - 1))
        if 0 <= t - 2 < pairs:
            commit(t - 2, res.pop(t - 2))

    gw = d_inner // SSM_GROUPS
    for g in range(SSM_GROUPS):
        sl = slice(g * gw, (g + 1) * gw)
        zg = z_ref[0, :, sl]
        yg = (yacc_ref[:, sl] + act_ref[:, sl] * dskip_ref[:, sl]) * (zg * _sigmoid(zg))
        yg = yg * lax.rsqrt(jnp.mean(yg * yg, axis=-1, keepdims=True) + EPS)
        y_ref[0, :, sl] = (yg * gnw_ref[:, sl]).astype(y_ref.dtype)

    @pl.when(c == pl.num_programs(1) - 1)
    def _():
        hout_ref[0] = state_ref[...]


def _ssd(xbc, z, dt, dtT, state, convw, convb, dtb, dtbT, alog, alogT, dskip, gnw, expand, *, q):
    b, l, conv_dim = xbc.shape
    d_inner = z.shape[2]
    nh = dt.shape[1]
    nc = l // q
    gn = SSM_GROUPS * SSM_STATE
    lanes = 2 * SSM_HEAD_DIM
    kern = functools.partial(_ssd_kernel, q=q, zero_init=state is None)
    state_specs = [] if state is None else [
        pl.BlockSpec((1, SSM_STATE, d_inner), lambda i, c: (i, 0, 0)),
        pl.BlockSpec((1, CARRY_ROWS, conv_dim), lambda i, c: (i, 0, 0))]
    return pl.pallas_call(
        kern,
        grid=(b, nc),
        in_specs=[pl.BlockSpec((1, q, conv_dim), lambda i, c: (i, c, 0)),
                  pl.BlockSpec((1, q, d_inner), lambda i, c: (i, c, 0)),
                  pl.BlockSpec((q, nh), lambda i, c: (i * nc + c, 0)),
                  pl.BlockSpec((1, nh, q), lambda i, c: (i * nc + c, 0, 0)),
                  *state_specs,
                  _resident(convw.shape), _resident(convb.shape),
                  _resident(dtb.shape), _resident(dtbT.shape),
                  _resident(alog.shape), _resident(alogT.shape),
                  _resident(dskip.shape), _resident(gnw.shape), _resident(expand.shape)],
        out_specs=[pl.BlockSpec((1, q, d_inner), lambda i, c: (i, c, 0)),
                   pl.BlockSpec((1, SSM_STATE, d_inner), lambda i, c: (i, 0, 0))],
        out_shape=[jax.ShapeDtypeStruct((b, l, d_inner), BF16),
                   jax.ShapeDtypeStruct((b, SSM_STATE, d_inner), F32)],
        scratch_shapes=[pltpu.VMEM((conv_dim // lanes, CARRY_ROWS + q, lanes), F32),
                        pltpu.VMEM((q, conv_dim), F32),
                        pltpu.VMEM((q, d_inner), BF16),
                        pltpu.VMEM((q, gn), BF16),
                        pltpu.VMEM((q, gn), BF16),
                        pltpu.VMEM((SSM_GROUPS, q, q), F32),
                        pltpu.VMEM((SSM_GROUPS, SSM_STATE, q), F32),
                        pltpu.VMEM((q, d_inner), F32),
                        pltpu.VMEM((SSM_STATE, d_inner), F32)],
        compiler_params=_cparams("parallel", "arbitrary"),
        name="ssd_mixer",
    )(xbc, z, dt, dtT, *(state or ()), convw, convb, dtb, dtbT, alog, alogT, dskip, gnw, expand)


def _mix_ffn_kernel(a_ref, wmix_ref, nw_mix_ref, x_ref, nw_in_ref, wg_ref, wu_ref, wo_ref,
                    nw_out_ref, o_ref, xn_ref, h_ref, *, hc, parts):
    tm = x_ref.shape[0]
    pr = tm // parts
    hidden = wg_ref.shape[1]
    chunks = list(range(0, hidden, hc))

    def mix(p):
        rows = slice(p * pr, (p + 1) * pr)
        x1 = x_ref[rows, :] + _rms(_dot(a_ref[rows, :], wmix_ref[...]), nw_mix_ref[...])
        o_ref[rows, :] = x1
        xn_ref[rows, :] = _rms(x1, nw_in_ref[...]).astype(BF16)

    def up(p, cs):
        rows = slice(p * pr, (p + 1) * pr)
        for c0 in cs:
            gate = _dot(xn_ref[rows, :], wg_ref[:, c0:c0 + hc])
            upv = _dot(xn_ref[rows, :], wu_ref[:, c0:c0 + hc])
            h_ref[rows, c0:c0 + hc] = (gate * _sigmoid(gate) * upv).astype(BF16)

    def down(p):
        rows = slice(p * pr, (p + 1) * pr)
        y = _dot(h_ref[rows, :], wo_ref[...])
        o_ref[rows, :] = o_ref[rows, :] + _rms(y, nw_out_ref[...])

    mix(0)
    for p in range(parts):
        up(p, chunks[:1])
        if p + 1 < parts:
            mix(p + 1)
        if p >= 1:
            down(p - 1)
        up(p, chunks[1:])
    down(parts - 1)


def _mix_ffn(a, wmix, nw_mix, x, nw_in, wg, wu, wo, nw_out, name):
    rows, d = x.shape
    k = a.shape[1]
    hidden = wg.shape[1]
    tm = ROW_TILE
    kern = functools.partial(_mix_ffn_kernel, hc=256, parts=2)
    return pl.pallas_call(
        kern,
        grid=(rows // tm,),
        in_specs=[pl.BlockSpec((tm, k), lambda i: (i, 0)),
                  _resident(wmix.shape), _resident(nw_mix.shape),
                  pl.BlockSpec((tm, d), lambda i: (i, 0)),
                  _resident(nw_in.shape), _resident(wg.shape), _resident(wu.shape),
                  _resident(wo.shape), _resident(nw_out.shape)],
        out_specs=pl.BlockSpec((tm, d), lambda i: (i, 0)),
        out_shape=jax.ShapeDtypeStruct((rows, d), F32),
        scratch_shapes=[pltpu.VMEM((tm, d), BF16), pltpu.VMEM((tm, hidden), BF16)],
        compiler_params=_cparams("parallel"),
        name=name,
    )(a, wmix, nw_mix, x, nw_in, wg, wu, wo, nw_out)


def _qkv_kernel(x_ref, nwq_ref, nwkv_ref, wq_ref, wk_ref, wv_ref,
                q_ref, k_ref, v_ref, kf_ref, vf_ref, *, pad_blocks):
    i = pl.program_id(1)

    @pl.when(i < pad_blocks)
    def _():
        k_ref[...] = jnp.zeros_like(k_ref)
        v_ref[...] = jnp.zeros_like(v_ref)

    @pl.when(i >= pad_blocks)
    def _():
        x = x_ref[0]
        xh = x * lax.rsqrt(jnp.mean(x * x, axis=-1, keepdims=True) + EPS)
        xq = (xh * nwq_ref[...]).astype(BF16)
        xkv = (xh * nwkv_ref[...]).astype(BF16)
        d = wq_ref.shape[1]
        for c0 in range(0, d, COL_TILE):
            sl = slice(c0, c0 + COL_TILE)
            q_ref[0, :, sl] = (_dot(xq, wq_ref[:, sl]) * Q_SCALE).astype(q_ref.dtype)
            kk = _dot(xkv, wk_ref[:, sl])
            vv = _dot(xkv, wv_ref[:, sl])
            k_ref[0, :, sl] = kk.astype(k_ref.dtype)
            v_ref[0, :, sl] = vv.astype(v_ref.dtype)
            kf_ref[0, :, sl] = kk
            vf_ref[0, :, sl] = vv


def _qkv(x, nwq, nwkv, wq, wk, wv, *, tm, pad_rows, tail_rows):
    b, l, d = x.shape
    pad_blocks = pad_rows // tm
    nblk = l // tm
    tail_blocks = tail_rows // tm
    kern = functools.partial(_qkv_kernel, pad_blocks=pad_blocks)
    src = lambda bi, i: (bi, jnp.maximum(i - pad_blocks, 0), 0)
    tail = lambda bi, i: (bi, jnp.maximum(i - pad_blocks - (nblk - tail_blocks), 0), 0)
    return pl.pallas_call(
        kern,
        grid=(b, nblk + pad_blocks),
        in_specs=[pl.BlockSpec((1, tm, d), src),
                  _resident(nwq.shape), _resident(nwkv.shape),
                  _resident(wq.shape), _resident(wk.shape), _resident(wv.shape)],
        out_specs=[pl.BlockSpec((1, tm, d), src),
                   pl.BlockSpec((1, tm, d), lambda bi, i: (bi, i, 0)),
                   pl.BlockSpec((1, tm, d), lambda bi, i: (bi, i, 0)),
                   pl.BlockSpec((1, tm, d), tail),
                   pl.BlockSpec((1, tm, d), tail)],
        out_shape=[jax.ShapeDtypeStruct((b, l, d), BF16),
                   jax.ShapeDtypeStruct((b, l + pad_rows, d), BF16),
                   jax.ShapeDtypeStruct((b, l + pad_rows, d), BF16),
                   jax.ShapeDtypeStruct((b, tail_rows, d), F32),
                   jax.ShapeDtypeStruct((b, tail_rows, d), F32)],
        compiler_params=_cparams("parallel", "arbitrary"),
        name="attn_qkv_proj",
    )(x, nwq, nwkv, wq, wk, wv)


def _bias_kernel(u_ref, o_ref):
    heads, width = u_ref.shape
    for h in range(heads):
        x = jnp.broadcast_to(u_ref[h:h + 1, :] * LOG2E, (CHUNK, width))
        r = pltpu.roll(x, width - CHUNK, 1, stride=1, stride_axis=0)
        o_ref[h] = r[:, :BAND]


def _bias_table(rel_bias):
    heads = rel_bias.shape[0]
    width = BAND + CHUNK
    far = jnp.broadcast_to(rel_bias[:, 2 * REL_CLIP:], (heads, width - (REL_CLIP + CHUNK)))
    near = jnp.flip(rel_bias[:, REL_CLIP - CHUNK + 1:], axis=1)
    u = jnp.concatenate([far, near], axis=1)
    return pl.pallas_call(
        _bias_kernel,
        out_shape=jax.ShapeDtypeStruct((heads, CHUNK, BAND), F32),
        name="attn_rel_bias_table",
    )(u)


def _attn_kernel(*refs, first_chunk, cps, with_cache):
    q_ref, k_ref, v_ref = refs[:3]
    ck_ref, cv_ref = refs[3:5] if with_cache else (None, None)
    bias_ref, o_ref, kbuf_ref, vbuf_ref = refs[5 if with_cache else 3:]
    step = pl.program_id(1)

    @pl.when(step == 0)
    def _():
        r = ck_ref.shape[1] if with_cache else 0
        if with_cache:
            kbuf_ref[0:r, :] = ck_ref[0].astype(BF16)
            vbuf_ref[0:r, :] = cv_ref[0].astype(BF16)
        kbuf_ref[r:, :] = k_ref[0]
        vbuf_ref[r:, :] = v_ref[0]

    pairs = bias_ref.shape[0]
    pw = 2 * ATT_HEAD_DIM
    left = lax.broadcasted_iota(jnp.int32, (CHUNK, pw), 1) < ATT_HEAD_DIM
    units = [(ci, hp) for ci in range(cps) for hp in range(pairs)]

    def body(masked):
        valid = {}
        if masked:
            j = lax.broadcasted_iota(jnp.int32, (2 * CHUNK, BAND), 1)
            for ci in range(cps):
                valid[ci] = j >= (BAND_CHUNKS - (step * cps + ci + first_chunk)) * CHUNK

        def band(ref, u):
            ci, hp = units[u]
            start = pl.multiple_of((step * cps + ci) * CHUNK, CHUNK)
            return ref[pl.ds(start, BAND), hp * pw:(hp + 1) * pw]

        def scores(u):
            ci, hp = units[u]
            qp = q_ref[0, ci * CHUNK:(ci + 1) * CHUNK, hp * pw:(hp + 1) * pw]
            zero = jnp.zeros_like(qp)
            q2 = jnp.concatenate([jnp.where(left, qp, zero), jnp.where(left, zero, qp)], axis=0)
            s = _dot_nt(q2, band(kbuf_ref, u)) + bias_ref[hp]
            return jnp.where(valid[ci], s, -jnp.inf) if masked else s

        def softmax(s):
            pr = jnp.exp2(s - jnp.max(s, axis=-1, keepdims=True))
            return pr.astype(BF16), jnp.sum(pr, axis=-1, keepdims=True)

        def weighted(u, pr):
            v = band(vbuf_ref, u)
            past = BAND - CHUNK
            return _dot(pr[:, :past], v[:past]) + _dot(pr[:, past:], v[past:])

        def finish(u, r, denom):
            ci, hp = units[u]
            r = r / denom
            o_ref[0, ci * CHUNK:(ci + 1) * CHUNK, hp * pw:(hp + 1) * pw] = (
                jnp.where(left, r[:CHUNK], r[CHUNK:]).astype(o_ref.dtype))

        n = len(units)
        s, pr, den, acc = {}, {}, {}, {}
        for t in range(n + 3):
            if t < n:
                s[t] = scores(t)
            if 0 <= t - 2 < n:
                acc[t - 2] = weighted(t - 2, pr.pop(t - 2))
            if 0 <= t - 1 < n:
                pr[t - 1], den[t - 1] = softmax(s.pop(t - 1))
            if 0 <= t - 3 < n:
                finish(t - 3, acc.pop(t - 3), den.pop(t - 3))

    if first_chunk >= BAND_CHUNKS:
        body(False)
    else:
        band_full = step * cps + first_chunk >= BAND_CHUNKS
        pl.when(band_full)(lambda: body(False))
        pl.when(jnp.logical_not(band_full))(lambda: body(True))


def _attention(q, k, v, bias, *, first_chunk, cps, cache=None):
    b, l, d = q.shape
    lk = k.shape[1]
    assert BAND_CHUNKS % cps == 0 and l % (cps * CHUNK) == 0
    kern = functools.partial(_attn_kernel, first_chunk=first_chunk, cps=cps,
                             with_cache=cache is not None)
    cache = list(cache or ())
    seq = lambda rows: pl.BlockSpec((1, rows, d), lambda i, c: (i, 0, 0))
    band_rows = lk + sum(a.shape[1] for a in cache[:1])
    return pl.pallas_call(
        kern,
        grid=(b, l // (cps * CHUNK)),
        in_specs=[pl.BlockSpec((1, cps * CHUNK, d), lambda i, c: (i, c, 0)),
                  seq(lk), seq(lk), *[seq(a.shape[1]) for a in cache],
                  _resident(bias.shape)],
        out_specs=pl.BlockSpec((1, cps * CHUNK, d), lambda i, c: (i, c, 0)),
        out_shape=jax.ShapeDtypeStruct((b, l, d), BF16),
        scratch_shapes=[pltpu.VMEM((band_rows, d), BF16), pltpu.VMEM((band_rows, d), BF16)],
        compiler_params=_cparams("parallel", "arbitrary"),
        name="band_attention",
    )(q, k, v, *cache, bias)


def _row(v):
    return v.reshape(1, -1).astype(F32)


def _ssd_layer(x3, state, p, nw, *, q):
    b, l, d = x3.shape
    rows = b * l
    d_inner = p["wz"].shape[1]
    conv_dim = p["wx"].shape[1]
    x = x3.reshape(rows, d)
    z, xbc, dt, dtT = _inproj(x, _row(nw[0]), p["wz"], p["wx"], p["wdt"], p["dtb"].shape[1])
    xbc3 = xbc.reshape(b, l, conv_dim)
    if state is not None:
        h0, conv_prev = state
        state = (jnp.transpose(h0, (0, 3, 1, 2)).reshape(b, SSM_STATE, d_inner),
                 jnp.pad(conv_prev, ((0, 0), (CARRY_ROWS - (CONV_W - 1), 0), (0, 0))))
    dtT = jnp.transpose(dtT.reshape(-1, rows // q, q), (1, 0, 2))
    y, hT = _ssd(xbc3, z.reshape(b, l, d_inner), dt, dtT, state,
                 p["convw"], p["convb"], p["dtb"], p["dtbT"], p["alog"], p["alogT"],
                 p["dskip"], p["gnw"], p["expand"], q=q)
    h_new = jnp.transpose(hT.reshape(b, SSM_STATE, -1, SSM_HEAD_DIM), (0, 2, 3, 1))
    new_conv = xbc3[:, l - (CONV_W - 1):]
    return y.reshape(rows, d_inner), h_new, new_conv


def _layer_tail(mixed, w_mix, x3, fp, nw, name):
    b, l, d = x3.shape
    out = _mix_ffn(mixed, w_mix, _row(nw[1]), x3.reshape(b * l, d), _row(nw[2]),
                   fp["wg"], fp["wu"], fp["wo"], _row(nw[3]), name)
    return out.reshape(b, l, d)


def kernel(x_prompt, x_sample, state_ssm, state_conv, cache_k, cache_v, norm_w,
           ssm_w_in, ssm_conv_w, ssm_conv_b, ssm_dt_bias, ssm_A_log, ssm_D, ssm_norm_w, ssm_w_out,
           kv_norm_w, w_kv, attn_w_q, attn_rel_bias, attn_w_o, ffn_w_in, ffn_w_out):
    bp, lp, d = x_prompt.shape
    bs, ls, _ = x_sample.shape
    n_a = ssm_w_in.shape[0]
    depth = norm_w.shape[0]
    d_inner = ssm_w_out.shape[1]
    heads = ssm_dt_bias.shape[1]
    conv_dim = ssm_conv_w.shape[2]
    att_dim = attn_w_q.shape[2]
    att_heads = att_dim // ATT_HEAD_DIM
    hidden = ffn_w_out.shape[1]

    xp, xs = x_prompt, x_sample
    ssm_p, conv_p, ssm_s, conv_s = [], [], [], []
    outs_kv = None
    for layer in range(depth):
        nw = norm_w[layer]
        fp = {"wg": ffn_w_in[layer][:, :hidden].astype(BF16),
              "wu": ffn_w_in[layer][:, hidden:].astype(BF16),
              "wo": ffn_w_out[layer].astype(BF16)}
        if layer < n_a:
            a = layer
            w_in = ssm_w_in[a]
            wdt = w_in[:, d_inner + conv_dim:]
            sp = {"wz": w_in[:, :d_inner].astype(BF16),
                  "wx": w_in[:, d_inner:d_inner + conv_dim].astype(BF16),
                  "wdt": jnp.pad(wdt, ((0, 0), (0, 2 * SSM_HEAD_DIM - heads))).astype(BF16),
                  "convw": ssm_conv_w[a], "convb": _row(ssm_conv_b[a]),
                  "dtb": _row(ssm_dt_bias[a]), "dtbT": ssm_dt_bias[a].reshape(-1, 1),
                  "alog": _row(ssm_A_log[a]), "alogT": ssm_A_log[a].reshape(-1, 1),
                  "dskip": _row(jnp.repeat(ssm_D[a], SSM_HEAD_DIM)),
                  "gnw": _row(ssm_norm_w[a]), "wout": ssm_w_out[a].astype(BF16),
                  "expand": jnp.repeat(jnp.eye(heads, dtype=BF16), SSM_HEAD_DIM, axis=1)}
            mp, hp_new, cp_new = _ssd_layer(xp, None, sp, nw, q=128)
            ms, hs_new, cs_new = _ssd_layer(xs, (state_ssm[a], state_conv[a]), sp, nw, q=ls)
            w_mix = sp["wout"]
            ssm_p.append(hp_new)
            conv_p.append(cp_new)
            ssm_s.append(hs_new)
            conv_s.append(cs_new)
        else:
            i = layer - n_a
            wq = attn_w_q[i].astype(BF16)
            wk = w_kv[:, :att_dim].astype(BF16)
            wv = w_kv[:, att_dim:].astype(BF16)
            w_mix = attn_w_o[i].astype(BF16)
            bias = _bias_table(attn_rel_bias[i]).reshape(att_heads // 2, 2 * CHUNK, BAND)
            rows_p = min(BAND_CHUNKS * CHUNK, lp)
            qp, kpb, vpb, kpf, vpf = _qkv(xp, _row(nw[0]), _row(kv_norm_w), wq, wk, wv,
                                          tm=ROW_TILE, pad_rows=BAND_CHUNKS * CHUNK,
                                          tail_rows=rows_p)
            mp = _attention(qp, kpb, vpb, bias, first_chunk=0,
                            cps=ATTN_CHUNKS_PER_STEP).reshape(bp * lp, att_dim)
            flat = lambda a: a.reshape(bs, ls, att_dim)
            qs, ksb, vsb, ksf, vsf = map(flat, _qkv(xs.reshape(1, bs * ls, d), _row(nw[0]),
                                                    _row(kv_norm_w), wq, wk, wv, tm=ROW_TILE,
                                                    pad_rows=0, tail_rows=bs * ls))
            r = cache_k.shape[1]
            cache = (cache_k.reshape(bs, r, att_dim), cache_v.reshape(bs, r, att_dim))
            ms = _attention(qs, ksb, vsb, bias, first_chunk=BAND_CHUNKS, cps=1,
                            cache=cache).reshape(bs * ls, att_dim)
            if outs_kv is None:
                outs_kv = (kpf.reshape(bp, rows_p, att_heads, ATT_HEAD_DIM),
                           vpf.reshape(bp, rows_p, att_heads, ATT_HEAD_DIM),
                           ksf.reshape(bs, ls, att_heads, ATT_HEAD_DIM),
                           vsf.reshape(bs, ls, att_heads, ATT_HEAD_DIM))
        xp = _layer_tail(mp, w_mix, xp, fp, nw, "mix_ffn_prompt")
        xs = _layer_tail(ms, w_mix, xs, fp, nw, "mix_ffn_sample")
    kp_out, vp_out, ks_out, vs_out = outs_kv
    return (xp, xs, jnp.stack(ssm_p), jnp.stack(conv_p), kp_out, vp_out,
            jnp.stack(ssm_s), jnp.stack(conv_s), ks_out, vs_out)
```

```python
import functools

import jax
import jax.numpy as jnp
from jax import lax
from jax.experimental import pallas as pl
from jax.experimental.pallas import tpu as pltpu

F32 = jnp.float32
BF16 = jnp.bfloat16

EPS = 1e-6
CHUNK = 64
BAND_CHUNKS = 8
BAND = (BAND_CHUNKS + 1) * CHUNK
REL_CLIP = 256
SSM_HEAD_DIM = 64
SSM_GROUPS = 4
SSM_STATE = 128
CONV_W = 4
ATT_HEAD_DIM = 64
LOG2E = 1.4426950408889634
Q_SCALE = ATT_HEAD_DIM ** -0.5 * LOG2E
CARRY_ROWS = 8

VMEM_LIMIT_BYTES = 56 * 1024 * 1024
ROW_TILE = 512
COL_TILE = 512
ATTN_CHUNKS_PER_STEP = 4


def _cparams(*sem):
    return pltpu.CompilerParams(dimension_semantics=sem,
                                vmem_limit_bytes=VMEM_LIMIT_BYTES)


def _resident(shape):
    zeros = (0,) * len(shape)
    return pl.BlockSpec(shape, lambda *_: zeros, pipeline_mode=pl.Buffered(1))


def _rms(x, w):
    ms = jnp.mean(x * x, axis=-1, keepdims=True)
    return x * lax.rsqrt(ms + EPS) * w


def _sigmoid(x):
    return 1.0 / (1.0 + jnp.exp2(x * -LOG2E))


def _silu_of_twice(h):
    return h + h * jnp.tanh(h)


def _softplus(x):
    return jnp.maximum(x, 0.0) + jnp.log(1.0 + jnp.exp(-jnp.abs(x)))


def _dot(a, b):
    return jnp.dot(a, b, preferred_element_type=F32)


def _dot_nt(a, b):
    return lax.dot_general(a, b, (((1,), (1,)), ((), ())), preferred_element_type=F32)


def _split3(x):
    hi = x.astype(BF16)
    r1 = x - hi.astype(F32)
    mid = r1.astype(BF16)
    lo = (r1 - mid.astype(F32)).astype(BF16)
    return hi, mid, lo


def _inproj_kernel(x_ref, nw_ref, wz_ref, wx_ref, wdt_ref,
                   z_ref, xbc_ref, dt_ref, dtT_ref):
    xn = _rms(x_ref[...], nw_ref[...]).astype(BF16)
    for w_ref, o_ref in ((wz_ref, z_ref), (wx_ref, xbc_ref)):
        n = w_ref.shape[1]
        for c0 in range(0, n, COL_TILE):
            o_ref[:, c0:c0 + COL_TILE] = _dot(xn, w_ref[:, c0:c0 + COL_TILE]).astype(o_ref.dtype)
    nh = dt_ref.shape[1]
    dt = _dot(xn, wdt_ref[...])
    dt_ref[...] = dt[:, :nh]
    dtT_ref[...] = dt.T[:nh, :]


def _inproj(x, nw, wz, wx, wdt, nh):
    rows, d = x.shape
    tm = ROW_TILE
    return pl.pallas_call(
        _inproj_kernel,
        grid=(rows // tm,),
        in_specs=[pl.BlockSpec((tm, d), lambda i: (i, 0)),
                  _resident(nw.shape), _resident(wz.shape), _resident(wx.shape),
                  _resident(wdt.shape)],
        out_specs=[pl.BlockSpec((tm, wz.shape[1]), lambda i: (i, 0)),
                   pl.BlockSpec((tm, wx.shape[1]), lambda i: (i, 0)),
                   pl.BlockSpec((tm, nh), lambda i: (i, 0)),
                   pl.BlockSpec((nh, tm), lambda i: (0, i))],
        out_shape=[jax.ShapeDtypeStruct((rows, wz.shape[1]), F32),
                   jax.ShapeDtypeStruct((rows, wx.shape[1]), F32),
                   jax.ShapeDtypeStruct((rows, nh), F32),
                   jax.ShapeDtypeStruct((nh, rows), F32)],
        compiler_params=_cparams("parallel"),
        name="ssm_in_proj",
    )(x, nw, wz, wx, wdt)


def _ssd_kernel(*refs, q, zero_init):
    xbc_ref, z_ref, dt_ref, dtT_ref = refs[:4]
    h0_ref, c0_ref = (None, None) if zero_init else refs[4:6]
    (convw_ref, convb_ref, dtb_ref, dtbT_ref, alog_ref, alogT_ref, dskip_ref, gnw_ref, expand_ref,
     y_ref, hout_ref,
     win_ref, act_ref, xsb_ref, bb_ref, cb_ref, cbm_ref, bT_ref, yacc_ref,
     state_ref) = refs[4 if zero_init else 6:]
    c = pl.program_id(1)
    d_inner = y_ref.shape[2]
    n = SSM_STATE
    gn = SSM_GROUPS * n
    lanes = 2 * SSM_HEAD_DIM
    pairs = d_inner // lanes
    pairs_per_group = pairs // SSM_GROUPS
    panels = win_ref.shape[0]
    b_off, c_off = d_inner, d_inner + gn

    @pl.when(c == 0)
    def _():
        if zero_init:
            state_ref[...] = jnp.zeros_like(state_ref)
            win_ref[:, 0:CARRY_ROWS, :] = jnp.zeros((panels, CARRY_ROWS, lanes), F32)
        else:
            state_ref[...] = h0_ref[0]
            for pn in range(panels):
                win_ref[pn, 0:CARRY_ROWS, :] = c0_ref[0, :, pn * lanes:(pn + 1) * lanes]

    for pn in range(panels):
        sl = slice(pn * lanes, (pn + 1) * lanes)
        win_ref[pn, CARRY_ROWS:CARRY_ROWS + q, :] = xbc_ref[0, :, sl]
        conv = convb_ref[:, sl]
        for k in range(CONV_W):
            r0 = CARRY_ROWS - (CONV_W - 1) + k
            conv = conv + win_ref[pn, r0:r0 + q, :] * convw_ref[k:k + 1, sl]
        win_ref[pn, 0:CARRY_ROWS, :] = win_ref[pn, q:q + CARRY_ROWS, :]
        act = _silu_of_twice(conv)
        act_ref[:, sl] = act
        if pn * lanes < b_off:
            xsb_ref[:, sl] = act.astype(BF16)
        elif pn * lanes < c_off:
            bb_ref[:, pn * lanes - b_off:(pn + 1) * lanes - b_off] = act.astype(BF16)
        else:
            cb_ref[:, pn * lanes - c_off:(pn + 1) * lanes - c_off] = act.astype(BF16)

    dt = _softplus(dt_ref[...] + dtb_ref[...])
    dtT = _softplus(dtT_ref[0] + dtbT_ref[...])
    dA = dt * (-LOG2E * jnp.exp(alog_ref[...]))
    dAT = dtT * (-LOG2E * jnp.exp(alogT_ref[...]))
    row = lax.broadcasted_iota(jnp.int32, (q, q), 0)
    col = lax.broadcasted_iota(jnp.int32, (q, q), 1)
    causal = row >= col
    tril = jnp.where(causal, 1.0, 0.0).astype(BF16)
    triu = jnp.where(row <= col, 1.0, 0.0).astype(BF16)
    acs = sum(_dot(tril, piece) for piece in _split3(dA))
    acsT = sum(_dot(piece, triu) for piece in _split3(dAT))
    a_lastT = acsT[:, q - 1:q]
    row_term = acsT - jnp.log(dtT) * LOG2E
    w_rows = dtT * jnp.exp2(a_lastT - acsT)
    d_last = jnp.broadcast_to(jnp.exp2(acs[q - 1:q, :]), (8, acs.shape[1]))
    d_lanes = sum(_dot(piece, expand_ref[...]) for piece in _split3(d_last))[0:1]

    for g in range(SSM_GROUPS):
        gs = slice(g * n, (g + 1) * n)
        cbm_ref[g] = _dot_nt(cb_ref[:, gs], bb_ref[:, gs])
        bT_ref[g] = act_ref[:, b_off + g * n:b_off + (g + 1) * n].T

    left_q = lax.broadcasted_iota(jnp.int32, (q, lanes), 1) < SSM_HEAD_DIM
    left_n = lax.broadcasted_iota(jnp.int32, (n, lanes), 1) < SSM_HEAD_DIM

    def operands(hp):
        g = hp // pairs_per_group
        sl = slice(hp * lanes, (hp + 1) * lanes)
        cbm = cbm_ref[g]
        cg32 = act_ref[:, c_off + g * n:c_off + (g + 1) * n]
        bgT = bT_ref[g]
        tops, btws = [], []
        for h in (2 * hp, 2 * hp + 1):
            a_b = jnp.broadcast_to(acs[:, h:h + 1], (q, n))
            inter = jnp.exp2(a_b) * cg32
            intra = cbm * jnp.exp2(jnp.where(causal, a_b[:, :q] - row_term[h:h + 1, :], -jnp.inf))
            tops.append(jnp.concatenate([inter.astype(BF16), intra.astype(BF16)], axis=1))
            btws.append((bgT * w_rows[h:h + 1, :]).astype(BF16))
        rhs = jnp.concatenate([state_ref[:, sl].astype(BF16), xsb_ref[:, sl]], axis=0)
        return jnp.concatenate(tops, axis=0), jnp.concatenate(btws, axis=0), rhs

    def products(ops):
        lhs_y, lhs_s, rhs = ops
        return _dot(lhs_y, rhs), _dot(lhs_s, rhs[n:])

    def commit(hp, res):
        res_y, res_s = res
        sl = slice(hp * lanes, (hp + 1) * lanes)
        yacc_ref[:, sl] = jnp.where(left_q, res_y[:q], res_y[q:])
        state_ref[:, sl] = state_ref[:, sl] * d_lanes[:, sl] + jnp.where(left_n, res_s[:n], res_s[n:])

    ops = {0: operands(0)}
    for t in range(1, pairs + 1):
        res = products(ops.pop(t - 1))
        if t < pairs:
            ops[t] = operands(t)
        commit(t - 1, res)

    gw = d_inner // SSM_GROUPS
    for g in range(SSM_GROUPS):
        sl = slice(g * gw, (g + 1) * gw)
        zg = z_ref[0, :, sl]
        yg = (yacc_ref[:, sl] + act_ref[:, sl] * dskip_ref[:, sl]) * _silu_of_twice(zg)
        yg = yg * lax.rsqrt(jnp.mean(yg * yg, axis=-1, keepdims=True) + EPS)
        y_ref[0, :, sl] = (yg * gnw_ref[:, sl]).astype(y_ref.dtype)

    @pl.when(c == pl.num_programs(1) - 1)
    def _():
        hout_ref[0] = state_ref[...]


def _ssd(xbc, z, dt, dtT, state, convw, convb, dtb, dtbT, alog, alogT, dskip, gnw, expand, *, q):
    b, l, conv_dim = xbc.shape
    d_inner = z.shape[2]
    nh = dt.shape[1]
    nc = l // q
    gn = SSM_GROUPS * SSM_STATE
    lanes = 2 * SSM_HEAD_DIM
    kern = functools.partial(_ssd_kernel, q=q, zero_init=state is None)
    state_specs = [] if state is None else [
        pl.BlockSpec((1, SSM_STATE, d_inner), lambda i, c: (i, 0, 0)),
        pl.BlockSpec((1, CARRY_ROWS, conv_dim), lambda i, c: (i, 0, 0))]
    return pl.pallas_call(
        kern,
        grid=(b, nc),
        in_specs=[pl.BlockSpec((1, q, conv_dim), lambda i, c: (i, c, 0)),
                  pl.BlockSpec((1, q, d_inner), lambda i, c: (i, c, 0)),
                  pl.BlockSpec((q, nh), lambda i, c: (i * nc + c, 0)),
                  pl.BlockSpec((1, nh, q), lambda i, c: (i * nc + c, 0, 0)),
                  *state_specs,
                  _resident(convw.shape), _resident(convb.shape),
                  _resident(dtb.shape), _resident(dtbT.shape),
                  _resident(alog.shape), _resident(alogT.shape),
                  _resident(dskip.shape), _resident(gnw.shape), _resident(expand.shape)],
        out_specs=[pl.BlockSpec((1, q, d_inner), lambda i, c: (i, c, 0)),
                   pl.BlockSpec((1, SSM_STATE, d_inner), lambda i, c: (i, 0, 0))],
        out_shape=[jax.ShapeDtypeStruct((b, l, d_inner), BF16),
                   jax.ShapeDtypeStruct((b, SSM_STATE, d_inner), F32)],
        scratch_shapes=[pltpu.VMEM((conv_dim // lanes, CARRY_ROWS + q, lanes), F32),
                        pltpu.VMEM((q, conv_dim), F32),
                        pltpu.VMEM((q, d_inner), BF16),
                        pltpu.VMEM((q, gn), BF16),
                        pltpu.VMEM((q, gn), BF16),
                        pltpu.VMEM((SSM_GROUPS, q, q), F32),
                        pltpu.VMEM((SSM_GROUPS, SSM_STATE, q), F32),
                        pltpu.VMEM((q, d_inner), F32),
                        pltpu.VMEM((SSM_STATE, d_inner), F32)],
        compiler_params=_cparams("parallel", "arbitrary"),
        name="ssd_mixer",
    )(xbc, z, dt, dtT, *(state or ()), convw, convb, dtb, dtbT, alog, alogT, dskip, gnw, expand)


def _mix_ffn_kernel(a_ref, wmix_ref, nw_mix_ref, x_ref, nw_in_ref, wg_ref, wu_ref, wo_ref,
                    nw_out_ref, o_ref, xn_ref, h_ref, *, hc, parts):
    tm = x_ref.shape[0]
    pr = tm // parts
    hidden = wg_ref.shape[1]
    chunks = list(range(0, hidden, hc))

    def mix(p):
        rows = slice(p * pr, (p + 1) * pr)
        x1 = x_ref[rows, :] + _rms(_dot(a_ref[rows, :], wmix_ref[...]), nw_mix_ref[...])
        o_ref[rows, :] = x1
        xn_ref[rows, :] = _rms(x1, nw_in_ref[...]).astype(BF16)

    def up(p, cs):
        rows = slice(p * pr, (p + 1) * pr)
        for c0 in cs:
            gate = _dot(xn_ref[rows, :], wg_ref[:, c0:c0 + hc])
            upv = _dot(xn_ref[rows, :], wu_ref[:, c0:c0 + hc])
            h_ref[rows, c0:c0 + hc] = (gate * _sigmoid(gate) * upv).astype(BF16)

    def down(p):
        rows = slice(p * pr, (p + 1) * pr)
        y = _dot(h_ref[rows, :], wo_ref[...])
        o_ref[rows, :] = o_ref[rows, :] + _rms(y, nw_out_ref[...])

    mix(0)
    for p in range(parts):
        up(p, chunks[:1])
        if p + 1 < parts:
            mix(p + 1)
        if p >= 1:
            down(p - 1)
        up(p, chunks[1:])
    down(parts - 1)


def _mix_ffn(a, wmix, nw_mix, x, nw_in, wg, wu, wo, nw_out, name):
    rows, d = x.shape
    k = a.shape[1]
    hidden = wg.shape[1]
    tm = ROW_TILE
    kern = functools.partial(_mix_ffn_kernel, hc=256, parts=2)
    return pl.pallas_call(
        kern,
        grid=(rows // tm,),
        in_specs=[pl.BlockSpec((tm, k), lambda i: (i, 0)),
                  _resident(wmix.shape), _resident(nw_mix.shape),
                  pl.BlockSpec((tm, d), lambda i: (i, 0)),
                  _resident(nw_in.shape), _resident(wg.shape), _resident(wu.shape),
                  _resident(wo.shape), _resident(nw_out.shape)],
        out_specs=pl.BlockSpec((tm, d), lambda i: (i, 0)),
        out_shape=jax.ShapeDtypeStruct((rows, d), F32),
        scratch_shapes=[pltpu.VMEM((tm, d), BF16), pltpu.VMEM((tm, hidden), BF16)],
        compiler_params=_cparams("parallel"),
        name=name,
    )(a, wmix, nw_mix, x, nw_in, wg, wu, wo, nw_out)


def _qkv_kernel(x_ref, nwq_ref, nwkv_ref, wq_ref, wk_ref, wv_ref,
                q_ref, k_ref, v_ref, kf_ref, vf_ref):
    x = x_ref[0]
    xh = x * lax.rsqrt(jnp.mean(x * x, axis=-1, keepdims=True) + EPS)
    xq = (xh * nwq_ref[...]).astype(BF16)
    xkv = (xh * nwkv_ref[...]).astype(BF16)
    d = wq_ref.shape[1]
    for c0 in range(0, d, COL_TILE):
        sl = slice(c0, c0 + COL_TILE)
        q_ref[0, :, sl] = (_dot(xq, wq_ref[:, sl]) * Q_SCALE).astype(q_ref.dtype)
        kk = _dot(xkv, wk_ref[:, sl])
        vv = _dot(xkv, wv_ref[:, sl])
        k_ref[0, :, sl] = kk.astype(k_ref.dtype)
        v_ref[0, :, sl] = vv.astype(v_ref.dtype)
        kf_ref[0, :, sl] = kk
        vf_ref[0, :, sl] = vv


def _qkv(x, nwq, nwkv, wq, wk, wv, *, tm, tail_rows):
    b, l, d = x.shape
    nblk = l // tm
    tail_blocks = tail_rows // tm
    rows = lambda bi, i: (bi, i, 0)
    tail = lambda bi, i: (bi, jnp.maximum(i - (nblk - tail_blocks), 0), 0)
    return pl.pallas_call(
        _qkv_kernel,
        grid=(b, nblk),
        in_specs=[pl.BlockSpec((1, tm, d), rows),
                  _resident(nwq.shape), _resident(nwkv.shape),
                  _resident(wq.shape), _resident(wk.shape), _resident(wv.shape)],
        out_specs=[pl.BlockSpec((1, tm, d), rows),
                   pl.BlockSpec((1, tm, d), rows),
                   pl.BlockSpec((1, tm, d), rows),
                   pl.BlockSpec((1, tm, d), tail),
                   pl.BlockSpec((1, tm, d), tail)],
        out_shape=[jax.ShapeDtypeStruct((b, l, d), BF16),
                   jax.ShapeDtypeStruct((b, l, d), BF16),
                   jax.ShapeDtypeStruct((b, l, d), BF16),
                   jax.ShapeDtypeStruct((b, tail_rows, d), F32),
                   jax.ShapeDtypeStruct((b, tail_rows, d), F32)],
        compiler_params=_cparams("parallel", "arbitrary"),
        name="attn_qkv_proj",
    )(x, nwq, nwkv, wq, wk, wv)


def _bias_kernel(u_ref, o_ref):
    heads, width = u_ref.shape
    for h in range(heads):
        x = jnp.broadcast_to(u_ref[h:h + 1, :] * LOG2E, (CHUNK, width))
        r = pltpu.roll(x, width - CHUNK, 1, stride=1, stride_axis=0)
        o_ref[h] = r[:, :BAND]


def _bias_table(rel_bias):
    heads = rel_bias.shape[0]
    width = BAND + CHUNK
    far = jnp.broadcast_to(rel_bias[:, 2 * REL_CLIP:], (heads, width - (REL_CLIP + CHUNK)))
    near = jnp.flip(rel_bias[:, REL_CLIP - CHUNK + 1:], axis=1)
    u = jnp.concatenate([far, near], axis=1)
    return pl.pallas_call(
        _bias_kernel,
        out_shape=jax.ShapeDtypeStruct((heads, CHUNK, BAND), F32),
        name="attn_rel_bias_table",
    )(u)


def _attn_kernel(*refs, first_chunk, cps, with_cache):
    q_ref, k_ref, v_ref = refs[:3]
    ck_ref, cv_ref = refs[3:5] if with_cache else (None, None)
    bias_ref, o_ref, kbuf_ref, vbuf_ref = refs[5 if with_cache else 3:]
    step = pl.program_id(1)

    @pl.when(step == 0)
    def _():
        r = CHUNK + (ck_ref.shape[1] if with_cache else 0)
        kbuf_ref[0:CHUNK, :] = jnp.zeros((CHUNK, kbuf_ref.shape[1]), BF16)
        vbuf_ref[0:CHUNK, :] = jnp.zeros((CHUNK, vbuf_ref.shape[1]), BF16)
        if with_cache:
            kbuf_ref[CHUNK:r, :] = ck_ref[0].astype(BF16)
            vbuf_ref[CHUNK:r, :] = cv_ref[0].astype(BF16)
        kbuf_ref[r:, :] = k_ref[0]
        vbuf_ref[r:, :] = v_ref[0]

    pairs = bias_ref.shape[0]
    pw = 2 * ATT_HEAD_DIM
    left = lax.broadcasted_iota(jnp.int32, (CHUNK, pw), 1) < ATT_HEAD_DIM
    units = [(ci, hp) for ci in range(cps) for hp in range(pairs)]

    def body(static_step):
        def geometry(u):
            if static_step is None:
                return BAND_CHUNKS + 1, False
            valid = min(static_step * cps + units[u][0] + first_chunk + 1, BAND_CHUNKS + 1)
            return valid + (1 - valid % 2), valid % 2 == 0

        def band(ref, u):
            ci, hp = units[u]
            take = geometry(u)[0] * CHUNK
            if static_step is None:
                first = (step * cps + ci + first_chunk - BAND_CHUNKS + 1) * CHUNK
                rows = pl.ds(pl.multiple_of(first, CHUNK), take)
            else:
                end = (static_step * cps + ci + first_chunk + 2) * CHUNK
                rows = slice(end - take, end)
            return ref[rows, hp * pw:(hp + 1) * pw]

        def scores(u):
            ci, hp = units[u]
            chunks, masked = geometry(u)
            qp = q_ref[0, ci * CHUNK:(ci + 1) * CHUNK, hp * pw:(hp + 1) * pw]
            zero = jnp.zeros_like(qp)
            q2 = jnp.concatenate([jnp.where(left, qp, zero), jnp.where(left, zero, qp)], axis=0)
            s = _dot_nt(q2, band(kbuf_ref, u)) + bias_ref[hp, :, BAND - chunks * CHUNK:]
            if masked:
                lane = lax.broadcasted_iota(jnp.int32, s.shape, 1)
                s = jnp.where(lane >= CHUNK, s, -jnp.inf)
            return s

        def softmax(s):
            pr = jnp.exp2(s - jnp.max(s, axis=-1, keepdims=True))
            return pr.astype(BF16), jnp.sum(pr, axis=-1, keepdims=True)

        def weighted(u, pr):
            v = band(vbuf_ref, u)
            past = (geometry(u)[0] - 1) * CHUNK
            own = _dot(pr[:, past:], v[past:])
            return own if past == 0 else _dot(pr[:, :past], v[:past]) + own

        def finish(u, r, denom):
            ci, hp = units[u]
            r = r / denom
            o_ref[0, ci * CHUNK:(ci + 1) * CHUNK, hp * pw:(hp + 1) * pw] = (
                jnp.where(left, r[:CHUNK], r[CHUNK:]).astype(o_ref.dtype))

        n = len(units)
        s, pr, den, acc = {}, {}, {}, {}
        for t in range(n + 3):
            if t < n:
                s[t] = scores(t)
            if 0 <= t - 2 < n:
                acc[t - 2] = weighted(t - 2, pr.pop(t - 2))
            if 0 <= t - 1 < n:
                pr[t - 1], den[t - 1] = softmax(s.pop(t - 1))
            if 0 <= t - 3 < n:
                finish(t - 3, acc.pop(t - 3), den.pop(t - 3))

    short_steps = max(0, BAND_CHUNKS - first_chunk) // cps
    for st in range(short_steps):
        pl.when(step == st)(functools.partial(body, st))
    pl.when(step >= short_steps)(functools.partial(body, None))


def _attention(q, k, v, bias, *, first_chunk, cps, cache=None):
    b, l, d = q.shape
    lk = k.shape[1]
    assert BAND_CHUNKS % cps == 0 and l % (cps * CHUNK) == 0
    kern = functools.partial(_attn_kernel, first_chunk=first_chunk, cps=cps,
                             with_cache=cache is not None)
    cache = list(cache or ())
    seq = lambda rows: pl.BlockSpec((1, rows, d), lambda i, c: (i, 0, 0))
    band_rows = CHUNK + lk + sum(a.shape[1] for a in cache[:1])
    return pl.pallas_call(
        kern,
        grid=(b, l // (cps * CHUNK)),
        in_specs=[pl.BlockSpec((1, cps * CHUNK, d), lambda i, c: (i, c, 0)),
                  seq(lk), seq(lk), *[seq(a.shape[1]) for a in cache],
                  _resident(bias.shape)],
        out_specs=pl.BlockSpec((1, cps * CHUNK, d), lambda i, c: (i, c, 0)),
        out_shape=jax.ShapeDtypeStruct((b, l, d), BF16),
        scratch_shapes=[pltpu.VMEM((band_rows, d), BF16), pltpu.VMEM((band_rows, d), BF16)],
        compiler_params=_cparams("parallel", "arbitrary"),
        name="band_attention",
    )(q, k, v, *cache, bias)


def _row(v):
    return v.reshape(1, -1).astype(F32)


def _ssd_layer(x3, state, p, nw, *, q):
    b, l, d = x3.shape
    rows = b * l
    d_inner = p["wz"].shape[1]
    conv_dim = p["wx"].shape[1]
    x = x3.reshape(rows, d)
    z, xbc, dt, dtT = _inproj(x, _row(nw[0]), p["wz"], p["wx"], p["wdt"], p["dtb"].shape[1])
    xbc3 = xbc.reshape(b, l, conv_dim)
    if state is not None:
        h0, conv_prev = state
        state = (jnp.transpose(h0, (0, 3, 1, 2)).reshape(b, SSM_STATE, d_inner),
                 jnp.pad(conv_prev, ((0, 0), (CARRY_ROWS - (CONV_W - 1), 0), (0, 0))))
    dtT = jnp.transpose(dtT.reshape(-1, rows // q, q), (1, 0, 2))
    y, hT = _ssd(xbc3, z.reshape(b, l, d_inner), dt, dtT, state,
                 p["convw"], p["convb"], p["dtb"], p["dtbT"], p["alog"], p["alogT"],
                 p["dskip"], p["gnw"], p["expand"], q=q)
    h_new = jnp.transpose(hT.reshape(b, SSM_STATE, -1, SSM_HEAD_DIM), (0, 2, 3, 1))
    new_conv = xbc3[:, l - (CONV_W - 1):]
    return y.reshape(rows, d_inner), h_new, new_conv


def _layer_tail(mixed, w_mix, x3, fp, nw, name):
    b, l, d = x3.shape
    out = _mix_ffn(mixed, w_mix, _row(nw[1]), x3.reshape(b * l, d), _row(nw[2]),
                   fp["wg"], fp["wu"], fp["wo"], _row(nw[3]), name)
    return out.reshape(b, l, d)


def kernel(x_prompt, x_sample, state_ssm, state_conv, cache_k, cache_v, norm_w,
           ssm_w_in, ssm_conv_w, ssm_conv_b, ssm_dt_bias, ssm_A_log, ssm_D, ssm_norm_w, ssm_w_out,
           kv_norm_w, w_kv, attn_w_q, attn_rel_bias, attn_w_o, ffn_w_in, ffn_w_out):
    bp, lp, d = x_prompt.shape
    bs, ls, _ = x_sample.shape
    n_a = ssm_w_in.shape[0]
    depth = norm_w.shape[0]
    d_inner = ssm_w_out.shape[1]
    heads = ssm_dt_bias.shape[1]
    conv_dim = ssm_conv_w.shape[2]
    att_dim = attn_w_q.shape[2]
    att_heads = att_dim // ATT_HEAD_DIM
    hidden = ffn_w_out.shape[1]

    xp, xs = x_prompt, x_sample
    ssm_p, conv_p, ssm_s, conv_s = [], [], [], []
    outs_kv = None
    for layer in range(depth):
        nw = norm_w[layer]
        fp = {"wg": ffn_w_in[layer][:, :hidden].astype(BF16),
              "wu": ffn_w_in[layer][:, hidden:].astype(BF16),
              "wo": ffn_w_out[layer].astype(BF16)}
        if layer < n_a:
            a = layer
            w_in = ssm_w_in[a]
            wdt = w_in[:, d_inner + conv_dim:]
            sp = {"wz": (0.5 * w_in[:, :d_inner]).astype(BF16),
                  "wx": w_in[:, d_inner:d_inner + conv_dim].astype(BF16),
                  "wdt": jnp.pad(wdt, ((0, 0), (0, 2 * SSM_HEAD_DIM - heads))).astype(BF16),
                  "convw": 0.5 * ssm_conv_w[a], "convb": _row(0.5 * ssm_conv_b[a]),
                  "dtb": _row(ssm_dt_bias[a]), "dtbT": ssm_dt_bias[a].reshape(-1, 1),
                  "alog": _row(ssm_A_log[a]), "alogT": ssm_A_log[a].reshape(-1, 1),
                  "dskip": _row(jnp.repeat(ssm_D[a], SSM_HEAD_DIM)),
                  "gnw": _row(ssm_norm_w[a]), "wout": ssm_w_out[a].astype(BF16),
                  "expand": jnp.repeat(jnp.eye(heads, dtype=BF16), SSM_HEAD_DIM, axis=1)}
            mp, hp_new, cp_new = _ssd_layer(xp, None, sp, nw, q=128)
            ms, hs_new, cs_new = _ssd_layer(xs, (state_ssm[a], state_conv[a]), sp, nw, q=ls)
            w_mix = sp["wout"]
            ssm_p.append(hp_new)
            conv_p.append(cp_new)
            ssm_s.append(hs_new)
            conv_s.append(cs_new)
        else:
            i = layer - n_a
            wq = attn_w_q[i].astype(BF16)
            wk = w_kv[:, :att_dim].astype(BF16)
            wv = w_kv[:, att_dim:].astype(BF16)
            w_mix = attn_w_o[i].astype(BF16)
            bias = _bias_table(attn_rel_bias[i]).reshape(att_heads // 2, 2 * CHUNK, BAND)
            rows_p = min(BAND_CHUNKS * CHUNK, lp)
            qp, kpb, vpb, kpf, vpf = _qkv(xp, _row(nw[0]), _row(kv_norm_w), wq, wk, wv,
                                          tm=ROW_TILE, tail_rows=rows_p)
            mp = _attention(qp, kpb, vpb, bias, first_chunk=0,
                            cps=ATTN_CHUNKS_PER_STEP).reshape(bp * lp, att_dim)
            flat = lambda a: a.reshape(bs, ls, att_dim)
            qs, ksb, vsb, ksf, vsf = map(flat, _qkv(xs.reshape(1, bs * ls, d), _row(nw[0]),
                                                    _row(kv_norm_w), wq, wk, wv, tm=ROW_TILE,
                                                    tail_rows=bs * ls))
            r = cache_k.shape[1]
            cache = (cache_k.astype(BF16).reshape(bs, r, att_dim),
                     cache_v.astype(BF16).reshape(bs, r, att_dim))
            ms = _attention(qs, ksb, vsb, bias, first_chunk=BAND_CHUNKS, cps=1,
                            cache=cache).reshape(bs * ls, att_dim)
            if outs_kv is None:
                outs_kv = (kpf.reshape(bp, rows_p, att_heads, ATT_HEAD_DIM),
                           vpf.reshape(bp, rows_p, att_heads, ATT_HEAD_DIM),
                           ksf.reshape(bs, ls, att_heads, ATT_HEAD_DIM),
                           vsf.reshape(bs, ls, att_heads, ATT_HEAD_DIM))
        xp = _layer_tail(mp, w_mix, xp, fp, nw, "mix_ffn_prompt")
        xs = _layer_tail(ms, w_mix, xs, fp, nw, "mix_ffn_sample")
    kp_out, vp_out, ks_out, vs_out = outs_kv
    return (xp, xs, jnp.stack(ssm_p), jnp.stack(conv_p), kp_out, vp_out,
            jnp.stack(ssm_s), jnp.stack(conv_s), ks_out, vs_out)
```

```python
import functools

import jax
import jax.numpy as jnp
from jax import lax
from jax.experimental import pallas as pl
from jax.experimental.pallas import tpu as pltpu

F32 = jnp.float32
BF16 = jnp.bfloat16

EPS = 1e-6
CHUNK = 64
BAND_CHUNKS = 8
BAND = (BAND_CHUNKS + 1) * CHUNK
REL_CLIP = 256
SSM_HEAD_DIM = 64
SSM_GROUPS = 4
SSM_STATE = 128
CONV_W = 4
ATT_HEAD_DIM = 64
LOG2E = 1.4426950408889634
Q_SCALE = ATT_HEAD_DIM ** -0.5 * LOG2E
CARRY_ROWS = 8

VMEM_LIMIT_BYTES = 56 * 1024 * 1024
ROW_TILE = 512
COL_TILE = 512
ATTN_CHUNKS_PER_STEP = 4


def _cparams(*sem):
    return pltpu.CompilerParams(dimension_semantics=sem,
                                vmem_limit_bytes=VMEM_LIMIT_BYTES)


def _resident(shape):
    zeros = (0,) * len(shape)
    return pl.BlockSpec(shape, lambda *_: zeros, pipeline_mode=pl.Buffered(1))


def _rms(x, w):
    ms = jnp.mean(x * x, axis=-1, keepdims=True)
    return x * lax.rsqrt(ms + EPS) * w


def _sigmoid(x):
    return 1.0 / (1.0 + jnp.exp2(x * -LOG2E))


def _silu_of_twice(h):
    return h + h * jnp.tanh(h)


def _softplus(x):
    return jnp.maximum(x, 0.0) + jnp.log(1.0 + jnp.exp(-jnp.abs(x)))


def _dot(a, b):
    return jnp.dot(a, b, preferred_element_type=F32)


def _dot_nt(a, b):
    return lax.dot_general(a, b, (((1,), (1,)), ((), ())), preferred_element_type=F32)


def _split3(x):
    hi = x.astype(BF16)
    r1 = x - hi.astype(F32)
    mid = r1.astype(BF16)
    lo = (r1 - mid.astype(F32)).astype(BF16)
    return hi, mid, lo


def _inproj_kernel(x_ref, nw_ref, w_ref, z_ref, xbc_ref, dt_ref, dtT_ref):
    xn = _rms(x_ref[...], nw_ref[...]).astype(BF16)
    d_inner, conv_dim, nh = z_ref.shape[1], xbc_ref.shape[1], dt_ref.shape[1]
    for c0 in range(0, d_inner, COL_TILE):
        z_ref[:, c0:c0 + COL_TILE] = 0.5 * _dot(xn, w_ref[:, c0:c0 + COL_TILE])
    for c0 in range(0, conv_dim, COL_TILE):
        xbc_ref[:, c0:c0 + COL_TILE] = _dot(xn, w_ref[:, d_inner + c0:d_inner + c0 + COL_TILE])
    dt = _dot(xn, w_ref[:, d_inner + conv_dim:])
    dt_ref[...] = dt[:, :nh]
    dtT_ref[...] = dt.T[:nh, :]


def _inproj(x, nw, w, *, d_inner, conv_dim, nh):
    rows, d = x.shape
    tm = ROW_TILE
    return pl.pallas_call(
        _inproj_kernel,
        grid=(rows // tm,),
        in_specs=[pl.BlockSpec((tm, d), lambda i: (i, 0)),
                  _resident(nw.shape), _resident(w.shape)],
        out_specs=[pl.BlockSpec((tm, d_inner), lambda i: (i, 0)),
                   pl.BlockSpec((tm, conv_dim), lambda i: (i, 0)),
                   pl.BlockSpec((tm, nh), lambda i: (i, 0)),
                   pl.BlockSpec((nh, tm), lambda i: (0, i))],
        out_shape=[jax.ShapeDtypeStruct((rows, d_inner), F32),
                   jax.ShapeDtypeStruct((rows, conv_dim), F32),
                   jax.ShapeDtypeStruct((rows, nh), F32),
                   jax.ShapeDtypeStruct((nh, rows), F32)],
        compiler_params=_cparams("parallel"),
        name="ssm_in_proj",
    )(x, nw, w)


def _ssd_kernel(*refs, q, zero_init):
    xbc_ref, z_ref, dt_ref, dtT_ref = refs[:4]
    h0_ref, c0_ref = (None, None) if zero_init else refs[4:6]
    (convw_ref, convb_ref, dtb_ref, dtbT_ref, alog_ref, alogT_ref, dskip_ref, gnw_ref, expand_ref,
     y_ref, hout_ref,
     win_ref, act_ref, xsb_ref, bb_ref, cb_ref, cbm_ref, bT_ref, yacc_ref,
     state_ref) = refs[4 if zero_init else 6:]
    c = pl.program_id(1)
    d_inner = y_ref.shape[2]
    n = SSM_STATE
    gn = SSM_GROUPS * n
    lanes = 2 * SSM_HEAD_DIM
    pairs = d_inner // lanes
    pairs_per_group = pairs // SSM_GROUPS
    panels = win_ref.shape[0]
    b_off, c_off = d_inner, d_inner + gn

    @pl.when(c == 0)
    def _():
        if zero_init:
            state_ref[...] = jnp.zeros_like(state_ref)
            win_ref[:, 0:CARRY_ROWS, :] = jnp.zeros((panels, CARRY_ROWS, lanes), F32)
        else:
            for hp in range(pairs):
                state_ref[:, hp * lanes:(hp + 1) * lanes] = (
                    h0_ref[0, 2 * hp:2 * hp + 2].reshape(lanes, n).T)
            for pn in range(panels):
                win_ref[pn, 0:CARRY_ROWS, :] = c0_ref[0, :, pn * lanes:(pn + 1) * lanes]

    for pn in range(panels):
        sl = slice(pn * lanes, (pn + 1) * lanes)
        win_ref[pn, CARRY_ROWS:CARRY_ROWS + q, :] = xbc_ref[0, :, sl]
        conv = convb_ref[:, sl]
        for k in range(CONV_W):
            r0 = CARRY_ROWS - (CONV_W - 1) + k
            conv = conv + win_ref[pn, r0:r0 + q, :] * convw_ref[k:k + 1, sl]
        win_ref[pn, 0:CARRY_ROWS, :] = win_ref[pn, q:q + CARRY_ROWS, :]
        act = _silu_of_twice(conv)
        act_ref[:, sl] = act
        if pn * lanes < b_off:
            xsb_ref[:, sl] = act.astype(BF16)
        elif pn * lanes < c_off:
            bb_ref[:, pn * lanes - b_off:(pn + 1) * lanes - b_off] = act.astype(BF16)
        else:
            cb_ref[:, pn * lanes - c_off:(pn + 1) * lanes - c_off] = act.astype(BF16)

    dt = _softplus(dt_ref[...] + dtb_ref[...])
    dtT = _softplus(dtT_ref[0] + dtbT_ref[...])
    dA = dt * (-LOG2E * jnp.exp(alog_ref[...]))
    dAT = dtT * (-LOG2E * jnp.exp(alogT_ref[...]))
    row = lax.broadcasted_iota(jnp.int32, (q, q), 0)
    col = lax.broadcasted_iota(jnp.int32, (q, q), 1)
    causal = row >= col
    tril = jnp.where(causal, 1.0, 0.0).astype(BF16)
    triu = jnp.where(row <= col, 1.0, 0.0).astype(BF16)
    acs = sum(_dot(tril, piece) for piece in _split3(dA))
    acsT = sum(_dot(piece, triu) for piece in _split3(dAT))
    a_lastT = acsT[:, q - 1:q]
    row_term = acsT - jnp.log(dtT) * LOG2E
    w_rows = dtT * jnp.exp2(a_lastT - acsT)
    d_last = jnp.broadcast_to(jnp.exp2(acs[q - 1:q, :]), (8, acs.shape[1]))
    d_lanes = sum(_dot(piece, expand_ref[...]) for piece in _split3(d_last))[0:1]

    for g in range(SSM_GROUPS):
        gs = slice(g * n, (g + 1) * n)
        cbm_ref[g] = _dot_nt(cb_ref[:, gs], bb_ref[:, gs])
        bT_ref[g] = act_ref[:, b_off + g * n:b_off + (g + 1) * n].T

    left_q = lax.broadcasted_iota(jnp.int32, (q, lanes), 1) < SSM_HEAD_DIM
    left_n = lax.broadcasted_iota(jnp.int32, (n, lanes), 1) < SSM_HEAD_DIM

    def operands(hp):
        g = hp // pairs_per_group
        sl = slice(hp * lanes, (hp + 1) * lanes)
        cbm = cbm_ref[g]
        cg32 = act_ref[:, c_off + g * n:c_off + (g + 1) * n]
        bgT = bT_ref[g]
        tops, btws = [], []
        for h in (2 * hp, 2 * hp + 1):
            a_b = jnp.broadcast_to(acs[:, h:h + 1], (q, n))
            inter = jnp.exp2(a_b) * cg32
            intra = cbm * jnp.exp2(jnp.where(causal, a_b[:, :q] - row_term[h:h + 1, :], -jnp.inf))
            tops.append(jnp.concatenate([inter.astype(BF16), intra.astype(BF16)], axis=1))
            btws.append((bgT * w_rows[h:h + 1, :]).astype(BF16))
        rhs = jnp.concatenate([state_ref[:, sl].astype(BF16), xsb_ref[:, sl]], axis=0)
        return jnp.concatenate(tops, axis=0), jnp.concatenate(btws, axis=0), rhs

    def products(ops):
        lhs_y, lhs_s, rhs = ops
        return _dot(lhs_y, rhs), _dot(lhs_s, rhs[n:])

    def commit(hp, res):
        res_y, res_s = res
        sl = slice(hp * lanes, (hp + 1) * lanes)
        yacc_ref[:, sl] = jnp.where(left_q, res_y[:q], res_y[q:])
        state_ref[:, sl] = state_ref[:, sl] * d_lanes[:, sl] + jnp.where(left_n, res_s[:n], res_s[n:])

    ops = {0: operands(0)}
    for t in range(1, pairs + 1):
        res = products(ops.pop(t - 1))
        if t < pairs:
            ops[t] = operands(t)
        commit(t - 1, res)

    gw = d_inner // SSM_GROUPS
    for g in range(SSM_GROUPS):
        sl = slice(g * gw, (g + 1) * gw)
        zg = z_ref[0, :, sl]
        yg = (yacc_ref[:, sl] + act_ref[:, sl] * dskip_ref[:, sl]) * _silu_of_twice(zg)
        yg = yg * lax.rsqrt(jnp.mean(yg * yg, axis=-1, keepdims=True) + EPS)
        y_ref[0, :, sl] = (yg * gnw_ref[:, sl]).astype(y_ref.dtype)

    @pl.when(c == pl.num_programs(1) - 1)
    def _():
        for hp in range(pairs):
            hout_ref[0, 2 * hp:2 * hp + 2] = (
                state_ref[:, hp * lanes:(hp + 1) * lanes].T.reshape(2, SSM_HEAD_DIM, n))


def _ssd(xbc, z, dt, dtT, state, convw, convb, dtb, dtbT, alog, alogT, dskip, gnw, expand, *, q):
    b, l, conv_dim = xbc.shape
    d_inner = z.shape[2]
    nh = dt.shape[1]
    nc = l // q
    gn = SSM_GROUPS * SSM_STATE
    lanes = 2 * SSM_HEAD_DIM
    kern = functools.partial(_ssd_kernel, q=q, zero_init=state is None)
    hshape = (d_inner // SSM_HEAD_DIM, SSM_HEAD_DIM, SSM_STATE)
    hspec = pl.BlockSpec((1, *hshape), lambda i, c: (i, 0, 0, 0))
    state_specs = [] if state is None else [
        hspec, pl.BlockSpec((1, CARRY_ROWS, conv_dim), lambda i, c: (i, 0, 0))]
    return pl.pallas_call(
        kern,
        grid=(b, nc),
        in_specs=[pl.BlockSpec((1, q, conv_dim), lambda i, c: (i, c, 0)),
                  pl.BlockSpec((1, q, d_inner), lambda i, c: (i, c, 0)),
                  pl.BlockSpec((q, nh), lambda i, c: (i * nc + c, 0)),
                  pl.BlockSpec((1, nh, q), lambda i, c: (i * nc + c, 0, 0)),
                  *state_specs,
                  _resident(convw.shape), _resident(convb.shape),
                  _resident(dtb.shape), _resident(dtbT.shape),
                  _resident(alog.shape), _resident(alogT.shape),
                  _resident(dskip.shape), _resident(gnw.shape), _resident(expand.shape)],
        out_specs=[pl.BlockSpec((1, q, d_inner), lambda i, c: (i, c, 0)), hspec],
        out_shape=[jax.ShapeDtypeStruct((b, l, d_inner), BF16),
                   jax.ShapeDtypeStruct((b, *hshape), F32)],
        scratch_shapes=[pltpu.VMEM((conv_dim // lanes, CARRY_ROWS + q, lanes), F32),
                        pltpu.VMEM((q, conv_dim), F32),
                        pltpu.VMEM((q, d_inner), BF16),
                        pltpu.VMEM((q, gn), BF16),
                        pltpu.VMEM((q, gn), BF16),
                        pltpu.VMEM((SSM_GROUPS, q, q), F32),
                        pltpu.VMEM((SSM_GROUPS, SSM_STATE, q), F32),
                        pltpu.VMEM((q, d_inner), F32),
                        pltpu.VMEM((SSM_STATE, d_inner), F32)],
        compiler_params=_cparams("parallel", "arbitrary"),
        name="ssd_mixer",
    )(xbc, z, dt, dtT, *(state or ()), convw, convb, dtb, dtbT, alog, alogT, dskip, gnw, expand)


def _mix_ffn_kernel(a_ref, wmix_ref, nw_mix_ref, x_ref, nw_in_ref, win_ref, wo_ref,
                    nw_out_ref, o_ref, xn_ref, h_ref, *, hc, parts):
    tm = x_ref.shape[0]
    pr = tm // parts
    hidden = wo_ref.shape[0]
    chunks = list(range(0, hidden, hc))

    def mix(p):
        rows = slice(p * pr, (p + 1) * pr)
        x1 = x_ref[rows, :] + _rms(_dot(a_ref[rows, :], wmix_ref[...]), nw_mix_ref[...])
        o_ref[rows, :] = x1
        xn_ref[rows, :] = _rms(x1, nw_in_ref[...]).astype(BF16)

    def up(p, cs):
        rows = slice(p * pr, (p + 1) * pr)
        for c0 in cs:
            gate = _dot(xn_ref[rows, :], win_ref[:, c0:c0 + hc])
            upv = _dot(xn_ref[rows, :], win_ref[:, hidden + c0:hidden + c0 + hc])
            h_ref[rows, c0:c0 + hc] = (gate * _sigmoid(gate) * upv).astype(BF16)

    def down(p):
        rows = slice(p * pr, (p + 1) * pr)
        y = _dot(h_ref[rows, :], wo_ref[...])
        o_ref[rows, :] = o_ref[rows, :] + _rms(y, nw_out_ref[...])

    mix(0)
    for p in range(parts):
        up(p, chunks[:1])
        if p + 1 < parts:
            mix(p + 1)
        if p >= 1:
            down(p - 1)
        up(p, chunks[1:])
    down(parts - 1)


def _mix_ffn(a, wmix, nw_mix, x, nw_in, win, wo, nw_out, name):
    rows, d = x.shape
    k = a.shape[1]
    hidden = wo.shape[0]
    tm = ROW_TILE
    kern = functools.partial(_mix_ffn_kernel, hc=256, parts=2)
    return pl.pallas_call(
        kern,
        grid=(rows // tm,),
        in_specs=[pl.BlockSpec((tm, k), lambda i: (i, 0)),
                  _resident(wmix.shape), _resident(nw_mix.shape),
                  pl.BlockSpec((tm, d), lambda i: (i, 0)),
                  _resident(nw_in.shape), _resident(win.shape),
                  _resident(wo.shape), _resident(nw_out.shape)],
        out_specs=pl.BlockSpec((tm, d), lambda i: (i, 0)),
        out_shape=jax.ShapeDtypeStruct((rows, d), F32),
        scratch_shapes=[pltpu.VMEM((tm, d), BF16), pltpu.VMEM((tm, hidden), BF16)],
        compiler_params=_cparams("parallel"),
        name=name,
    )(a, wmix, nw_mix, x, nw_in, win, wo, nw_out)


def _qkv_kernel(x_ref, nwq_ref, nwkv_ref, wq_ref, wkv_ref,
                q_ref, k_ref, v_ref, kf_ref, vf_ref):
    x = x_ref[0]
    xh = x * lax.rsqrt(jnp.mean(x * x, axis=-1, keepdims=True) + EPS)
    xq = (xh * nwq_ref[...]).astype(BF16)
    xkv = (xh * nwkv_ref[...]).astype(BF16)
    d = wq_ref.shape[1]
    for c0 in range(0, d, COL_TILE):
        sl = slice(c0, c0 + COL_TILE)
        q_ref[0, :, sl] = (_dot(xq, wq_ref[:, sl]) * Q_SCALE).astype(q_ref.dtype)
        kk = _dot(xkv, wkv_ref[:, sl])
        vv = _dot(xkv, wkv_ref[:, d + c0:d + c0 + COL_TILE])
        k_ref[0, :, sl] = kk.astype(k_ref.dtype)
        v_ref[0, :, sl] = vv.astype(v_ref.dtype)
        kf_ref[0, :, sl] = kk
        vf_ref[0, :, sl] = vv


def _qkv(x, nwq, nwkv, wq, wkv, *, tm, tail_rows):
    b, l, d = x.shape
    nblk = l // tm
    tail_blocks = tail_rows // tm
    rows = lambda bi, i: (bi, i, 0)
    tail = lambda bi, i: (bi, jnp.maximum(i - (nblk - tail_blocks), 0), 0)
    return pl.pallas_call(
        _qkv_kernel,
        grid=(b, nblk),
        in_specs=[pl.BlockSpec((1, tm, d), rows),
                  _resident(nwq.shape), _resident(nwkv.shape),
                  _resident(wq.shape), _resident(wkv.shape)],
        out_specs=[pl.BlockSpec((1, tm, d), rows),
                   pl.BlockSpec((1, tm, d), rows),
                   pl.BlockSpec((1, tm, d), rows),
                   pl.BlockSpec((1, tm, d), tail),
                   pl.BlockSpec((1, tm, d), tail)],
        out_shape=[jax.ShapeDtypeStruct((b, l, d), BF16),
                   jax.ShapeDtypeStruct((b, l, d), BF16),
                   jax.ShapeDtypeStruct((b, l, d), BF16),
                   jax.ShapeDtypeStruct((b, tail_rows, d), F32),
                   jax.ShapeDtypeStruct((b, tail_rows, d), F32)],
        compiler_params=_cparams("parallel", "arbitrary"),
        name="attn_qkv_proj",
    )(x, nwq, nwkv, wq, wkv)


def _bias_kernel(u_ref, o_ref):
    heads, width = u_ref.shape
    for h in range(heads):
        x = jnp.broadcast_to(u_ref[h:h + 1, :] * LOG2E, (CHUNK, width))
        r = pltpu.roll(x, width - CHUNK, 1, stride=1, stride_axis=0)
        o_ref[h] = r[:, :BAND]


def _bias_table(rel_bias):
    heads = rel_bias.shape[0]
    width = BAND + CHUNK
    far = jnp.broadcast_to(rel_bias[:, 2 * REL_CLIP:], (heads, width - (REL_CLIP + CHUNK)))
    near = jnp.flip(rel_bias[:, REL_CLIP - CHUNK + 1:], axis=1)
    u = jnp.concatenate([far, near], axis=1)
    return pl.pallas_call(
        _bias_kernel,
        out_shape=jax.ShapeDtypeStruct((heads, CHUNK, BAND), F32),
        name="attn_rel_bias_table",
    )(u)


def _attn_kernel(*refs, first_chunk, cps, with_cache):
    q_ref, k_ref, v_ref = refs[:3]
    ck_ref, cv_ref = refs[3:5] if with_cache else (None, None)
    bias_ref, o_ref, kbuf_ref, vbuf_ref = refs[5 if with_cache else 3:]
    step = pl.program_id(1)

    @pl.when(step == 0)
    def _():
        r = CHUNK + (ck_ref.shape[1] if with_cache else 0)
        kbuf_ref[0:CHUNK, :] = jnp.zeros((CHUNK, kbuf_ref.shape[1]), BF16)
        vbuf_ref[0:CHUNK, :] = jnp.zeros((CHUNK, vbuf_ref.shape[1]), BF16)
        if with_cache:
            kbuf_ref[CHUNK:r, :] = ck_ref[0].astype(BF16)
            vbuf_ref[CHUNK:r, :] = cv_ref[0].astype(BF16)
        kbuf_ref[r:, :] = k_ref[0]
        vbuf_ref[r:, :] = v_ref[0]

    pairs = bias_ref.shape[0]
    pw = 2 * ATT_HEAD_DIM
    left = lax.broadcasted_iota(jnp.int32, (CHUNK, pw), 1) < ATT_HEAD_DIM
    units = [(ci, hp) for ci in range(cps) for hp in range(pairs)]

    def body(static_step):
        def geometry(u):
            if static_step is None:
                return BAND_CHUNKS + 1, False
            valid = min(static_step * cps + units[u][0] + first_chunk + 1, BAND_CHUNKS + 1)
            return valid + (1 - valid % 2), valid % 2 == 0

        def band(ref, u):
            ci, hp = units[u]
            take = geometry(u)[0] * CHUNK
            if static_step is None:
                first = (step * cps + ci + first_chunk - BAND_CHUNKS + 1) * CHUNK
                rows = pl.ds(pl.multiple_of(first, CHUNK), take)
            else:
                end = (static_step * cps + ci + first_chunk + 2) * CHUNK
                rows = slice(end - take, end)
            return ref[rows, hp * pw:(hp + 1) * pw]

        def scores(u):
            ci, hp = units[u]
            chunks, masked = geometry(u)
            qp = q_ref[0, ci * CHUNK:(ci + 1) * CHUNK, hp * pw:(hp + 1) * pw]
            zero = jnp.zeros_like(qp)
            q2 = jnp.concatenate([jnp.where(left, qp, zero), jnp.where(left, zero, qp)], axis=0)
            s = _dot_nt(q2, band(kbuf_ref, u)) + bias_ref[hp, :, BAND - chunks * CHUNK:]
            if masked:
                lane = lax.broadcasted_iota(jnp.int32, s.shape, 1)
                s = jnp.where(lane >= CHUNK, s, -jnp.inf)
            return s

        def softmax(s):
            pr = jnp.exp2(s - jnp.max(s, axis=-1, keepdims=True))
            return pr.astype(BF16), jnp.sum(pr, axis=-1, keepdims=True)

        def weighted(u, pr):
            v = band(vbuf_ref, u)
            past = (geometry(u)[0] - 1) * CHUNK
            own = _dot(pr[:, past:], v[past:])
            return own if past == 0 else _dot(pr[:, :past], v[:past]) + own

        def finish(u, r, denom):
            ci, hp = units[u]
            r = r / denom
            o_ref[0, ci * CHUNK:(ci + 1) * CHUNK, hp * pw:(hp + 1) * pw] = (
                jnp.where(left, r[:CHUNK], r[CHUNK:]).astype(o_ref.dtype))

        n = len(units)
        s, pr, den, acc = {}, {}, {}, {}
        for t in range(n + 3):
            if t < n:
                s[t] = scores(t)
            if 0 <= t - 2 < n:
                acc[t - 2] = weighted(t - 2, pr.pop(t - 2))
            if 0 <= t - 1 < n:
                pr[t - 1], den[t - 1] = softmax(s.pop(t - 1))
            if 0 <= t - 3 < n:
                finish(t - 3, acc.pop(t - 3), den.pop(t - 3))

    short_steps = max(0, BAND_CHUNKS - first_chunk) // cps
    for st in range(short_steps):
        pl.when(step == st)(functools.partial(body, st))
    pl.when(step >= short_steps)(functools.partial(body, None))


def _attention(q, k, v, bias, *, first_chunk, cps, cache=None):
    b, l, d = q.shape
    lk = k.shape[1]
    assert BAND_CHUNKS % cps == 0 and l % (cps * CHUNK) == 0
    kern = functools.partial(_attn_kernel, first_chunk=first_chunk, cps=cps,
                             with_cache=cache is not None)
    cache = list(cache or ())
    seq = lambda rows: pl.BlockSpec((1, rows, d), lambda i, c: (i, 0, 0))
    band_rows = CHUNK + lk + sum(a.shape[1] for a in cache[:1])
    return pl.pallas_call(
        kern,
        grid=(b, l // (cps * CHUNK)),
        in_specs=[pl.BlockSpec((1, cps * CHUNK, d), lambda i, c: (i, c, 0)),
                  seq(lk), seq(lk), *[seq(a.shape[1]) for a in cache],
                  _resident(bias.shape)],
        out_specs=pl.BlockSpec((1, cps * CHUNK, d), lambda i, c: (i, c, 0)),
        out_shape=jax.ShapeDtypeStruct((b, l, d), BF16),
        scratch_shapes=[pltpu.VMEM((band_rows, d), BF16), pltpu.VMEM((band_rows, d), BF16)],
        compiler_params=_cparams("parallel", "arbitrary"),
        name="band_attention",
    )(q, k, v, *cache, bias)


def _row(v):
    return v.reshape(1, -1).astype(F32)


def _ssd_layer(x3, state, p, nw, *, q):
    b, l, d = x3.shape
    rows = b * l
    d_inner = p["wout"].shape[0]
    conv_dim = p["convw"].shape[1]
    x = x3.reshape(rows, d)
    z, xbc, dt, dtT = _inproj(x, _row(nw[0]), p["win"], d_inner=d_inner, conv_dim=conv_dim,
                              nh=p["dtb"].shape[1])
    xbc3 = xbc.reshape(b, l, conv_dim)
    if state is not None:
        h0, conv_prev = state
        state = (h0, jnp.pad(conv_prev, ((0, 0), (CARRY_ROWS - (CONV_W - 1), 0), (0, 0))))
    dtT = jnp.transpose(dtT.reshape(-1, rows // q, q), (1, 0, 2))
    y, h_new = _ssd(xbc3, z.reshape(b, l, d_inner), dt, dtT, state,
                 p["convw"], p["convb"], p["dtb"], p["dtbT"], p["alog"], p["alogT"],
                 p["dskip"], p["gnw"], p["expand"], q=q)
    new_conv = xbc3[:, l - (CONV_W - 1):]
    return y.reshape(rows, d_inner), h_new, new_conv


def _layer_tail(mixed, w_mix, x3, fp, nw, name):
    b, l, d = x3.shape
    out = _mix_ffn(mixed, w_mix, _row(nw[1]), x3.reshape(b * l, d), _row(nw[2]),
                   fp["win"], fp["wo"], _row(nw[3]), name)
    return out.reshape(b, l, d)


def kernel(x_prompt, x_sample, state_ssm, state_conv, cache_k, cache_v, norm_w,
           ssm_w_in, ssm_conv_w, ssm_conv_b, ssm_dt_bias, ssm_A_log, ssm_D, ssm_norm_w, ssm_w_out,
           kv_norm_w, w_kv, attn_w_q, attn_rel_bias, attn_w_o, ffn_w_in, ffn_w_out):
    bp, lp, d = x_prompt.shape
    bs, ls, _ = x_sample.shape
    n_a = ssm_w_in.shape[0]
    depth = norm_w.shape[0]
    d_inner = ssm_w_out.shape[1]
    heads = ssm_dt_bias.shape[1]
    conv_dim = ssm_conv_w.shape[2]
    att_dim = attn_w_q.shape[2]
    att_heads = att_dim // ATT_HEAD_DIM
    hidden = ffn_w_out.shape[1]

    xp, xs = x_prompt, x_sample
    ssm_p, conv_p, ssm_s, conv_s = [], [], [], []
    outs_kv = None
    for layer in range(depth):
        nw = norm_w[layer]
        fp = {"win": ffn_w_in[layer].astype(BF16), "wo": ffn_w_out[layer].astype(BF16)}
        if layer < n_a:
            a = layer
            sp = {"win": jnp.pad(ssm_w_in[a], ((0, 0), (0, 2 * SSM_HEAD_DIM - heads))).astype(BF16),
                  "convw": 0.5 * ssm_conv_w[a], "convb": _row(0.5 * ssm_conv_b[a]),
                  "dtb": _row(ssm_dt_bias[a]), "dtbT": ssm_dt_bias[a].reshape(-1, 1),
                  "alog": _row(ssm_A_log[a]), "alogT": ssm_A_log[a].reshape(-1, 1),
                  "dskip": _row(jnp.repeat(ssm_D[a], SSM_HEAD_DIM)),
                  "gnw": _row(ssm_norm_w[a]), "wout": ssm_w_out[a].astype(BF16),
                  "expand": jnp.repeat(jnp.eye(heads, dtype=BF16), SSM_HEAD_DIM, axis=1)}
            mp, hp_new, cp_new = _ssd_layer(xp, None, sp, nw, q=128)
            ms, hs_new, cs_new = _ssd_layer(xs, (state_ssm[a], state_conv[a]), sp, nw, q=ls)
            w_mix = sp["wout"]
            ssm_p.append(hp_new)
            conv_p.append(cp_new)
            ssm_s.append(hs_new)
            conv_s.append(cs_new)
        else:
            i = layer - n_a
            wq = attn_w_q[i].astype(BF16)
            wkv = w_kv.astype(BF16)
            w_mix = attn_w_o[i].astype(BF16)
            bias = _bias_table(attn_rel_bias[i]).reshape(att_heads // 2, 2 * CHUNK, BAND)
            rows_p = min(BAND_CHUNKS * CHUNK, lp)
            qp, kpb, vpb, kpf, vpf = _qkv(xp, _row(nw[0]), _row(kv_norm_w), wq, wkv,
                                          tm=ROW_TILE, tail_rows=rows_p)
            mp = _attention(qp, kpb, vpb, bias, first_chunk=0,
                            cps=ATTN_CHUNKS_PER_STEP).reshape(bp * lp, att_dim)
            flat = lambda a: a.reshape(bs, ls, att_dim)
            qs, ksb, vsb, ksf, vsf = map(flat, _qkv(xs.reshape(1, bs * ls, d), _row(nw[0]),
                                                    _row(kv_norm_w), wq, wkv, tm=ROW_TILE,
                                                    tail_rows=bs * ls))
            r = cache_k.shape[1]
            cache = (cache_k.astype(BF16).reshape(bs, r, att_dim),
                     cache_v.astype(BF16).reshape(bs, r, att_dim))
            ms = _attention(qs, ksb, vsb, bias, first_chunk=BAND_CHUNKS, cps=1,
                            cache=cache).reshape(bs * ls, att_dim)
            if outs_kv is None:
                outs_kv = (kpf.reshape(bp, rows_p, att_heads, ATT_HEAD_DIM),
                           vpf.reshape(bp, rows_p, att_heads, ATT_HEAD_DIM),
                           ksf.reshape(bs, ls, att_heads, ATT_HEAD_DIM),
                           vsf.reshape(bs, ls, att_heads, ATT_HEAD_DIM))
        xp = _layer_tail(mp, w_mix, xp, fp, nw, "mix_ffn_prompt")
        xs = _layer_tail(ms, w_mix, xs, fp, nw, "mix_ffn_sample")
    kp_out, vp_out, ks_out, vs_out = outs_kv
    return (xp, xs, jnp.stack(ssm_p), jnp.stack(conv_p), kp_out, vp_out,
            jnp.stack(ssm_s), jnp.stack(conv_s), ks_out, vs_out)
```

```python
import functools

import jax
import jax.numpy as jnp
from jax import lax
from jax.experimental import pallas as pl
from jax.experimental.pallas import tpu as pltpu

F32 = jnp.float32
BF16 = jnp.bfloat16

EPS = 1e-6
CHUNK = 64
BAND_CHUNKS = 8
BAND = (BAND_CHUNKS + 1) * CHUNK
REL_CLIP = 256
SSM_HEAD_DIM = 64
SSM_GROUPS = 4
SSM_STATE = 128
CONV_W = 4
ATT_HEAD_DIM = 64
LOG2E = 1.4426950408889634
Q_SCALE = ATT_HEAD_DIM ** -0.5 * LOG2E
CARRY_ROWS = 8

VMEM_LIMIT_BYTES = 56 * 1024 * 1024
ROW_TILE = 512
COL_TILE = 512
ATTN_CHUNKS_PER_STEP = 4
SSD_BLOCK = 128
SSD_BLOCKS_PER_STEP = 4


def _cparams(*sem):
    return pltpu.CompilerParams(dimension_semantics=sem,
                                vmem_limit_bytes=VMEM_LIMIT_BYTES)


def _resident(shape):
    zeros = (0,) * len(shape)
    return pl.BlockSpec(shape, lambda *_: zeros, pipeline_mode=pl.Buffered(1))


def _rms(x, w):
    ms = jnp.mean(x * x, axis=-1, keepdims=True)
    return x * lax.rsqrt(ms + EPS) * w


def _sigmoid(x):
    return 1.0 / (1.0 + jnp.exp2(x * -LOG2E))


def _silu_of_twice(h):
    return h + h * jnp.tanh(h)


def _softplus(x):
    return jnp.maximum(x, 0.0) + jnp.log(1.0 + jnp.exp(-jnp.abs(x)))


def _dot(a, b):
    return jnp.dot(a, b, preferred_element_type=F32)


def _dot_nt(a, b):
    return lax.dot_general(a, b, (((1,), (1,)), ((), ())), preferred_element_type=F32)


def _split3(x):
    hi = x.astype(BF16)
    r1 = x - hi.astype(F32)
    mid = r1.astype(BF16)
    lo = (r1 - mid.astype(F32)).astype(BF16)
    return hi, mid, lo


def _inproj_kernel(x_ref, nw_ref, w_ref, z_ref, xbc_ref, dt_ref, dtT_ref):
    xn = _rms(x_ref[...], nw_ref[...]).astype(BF16)
    d_inner, conv_dim, nh = z_ref.shape[1], xbc_ref.shape[1], dt_ref.shape[1]
    for c0 in range(0, d_inner, COL_TILE):
        z_ref[:, c0:c0 + COL_TILE] = 0.5 * _dot(xn, w_ref[:, c0:c0 + COL_TILE])
    for c0 in range(0, conv_dim, COL_TILE):
        xbc_ref[:, c0:c0 + COL_TILE] = _dot(xn, w_ref[:, d_inner + c0:d_inner + c0 + COL_TILE])
    dt = _dot(xn, w_ref[:, d_inner + conv_dim:])
    dt_ref[...] = dt[:, :nh]
    dtT_ref[...] = dt.T[:nh, :]


def _inproj(x, nw, w, *, d_inner, conv_dim, nh):
    rows, d = x.shape
    tm = ROW_TILE
    return pl.pallas_call(
        _inproj_kernel,
        grid=(rows // tm,),
        in_specs=[pl.BlockSpec((tm, d), lambda i: (i, 0)),
                  _resident(nw.shape), _resident(w.shape)],
        out_specs=[pl.BlockSpec((tm, d_inner), lambda i: (i, 0)),
                   pl.BlockSpec((tm, conv_dim), lambda i: (i, 0)),
                   pl.BlockSpec((tm, nh), lambda i: (i, 0)),
                   pl.BlockSpec((nh, tm), lambda i: (0, i))],
        out_shape=[jax.ShapeDtypeStruct((rows, d_inner), F32),
                   jax.ShapeDtypeStruct((rows, conv_dim), F32),
                   jax.ShapeDtypeStruct((rows, nh), F32),
                   jax.ShapeDtypeStruct((nh, rows), F32)],
        compiler_params=_cparams("parallel"),
        name="ssm_in_proj",
    )(x, nw, w)


def _ssd_block(j, q, xbc_ref, z_ref, dt_ref, dtT_ref,
               convw_ref, convb_ref, dtb_ref, dtbT_ref, alog_ref, alogT_ref, dskip_ref, gnw_ref,
               expand_ref, y_ref,
               win_ref, act_ref, xsb_ref, bb_ref, cb_ref, cbm_ref, bT_ref, yacc_ref, state_ref):
    rows = slice(j * q, (j + 1) * q)
    d_inner = y_ref.shape[2]
    n = SSM_STATE
    gn = SSM_GROUPS * n
    lanes = 2 * SSM_HEAD_DIM
    pairs = d_inner // lanes
    pairs_per_group = pairs // SSM_GROUPS
    panels = win_ref.shape[0]
    b_off, c_off = d_inner, d_inner + gn

    for pn in range(panels):
        sl = slice(pn * lanes, (pn + 1) * lanes)
        win_ref[pn, CARRY_ROWS:CARRY_ROWS + q, :] = xbc_ref[0, rows, sl]
        conv = convb_ref[:, sl]
        for k in range(CONV_W):
            r0 = CARRY_ROWS - (CONV_W - 1) + k
            conv = conv + win_ref[pn, r0:r0 + q, :] * convw_ref[k:k + 1, sl]
        win_ref[pn, 0:CARRY_ROWS, :] = win_ref[pn, q:q + CARRY_ROWS, :]
        act = _silu_of_twice(conv)
        act_ref[:, sl] = act
        if pn * lanes < b_off:
            xsb_ref[:, sl] = act.astype(BF16)
        elif pn * lanes < c_off:
            bb_ref[:, pn * lanes - b_off:(pn + 1) * lanes - b_off] = act.astype(BF16)
        else:
            cb_ref[:, pn * lanes - c_off:(pn + 1) * lanes - c_off] = act.astype(BF16)

    dt = _softplus(dt_ref[rows, :] + dtb_ref[...])
    dtT = _softplus(dtT_ref[j] + dtbT_ref[...])
    dA = dt * (-LOG2E * jnp.exp(alog_ref[...]))
    dAT = dtT * (-LOG2E * jnp.exp(alogT_ref[...]))
    row = lax.broadcasted_iota(jnp.int32, (q, q), 0)
    col = lax.broadcasted_iota(jnp.int32, (q, q), 1)
    causal = row >= col
    tril = jnp.where(causal, 1.0, 0.0).astype(BF16)
    triu = jnp.where(row <= col, 1.0, 0.0).astype(BF16)
    acs = sum(_dot(tril, piece) for piece in _split3(dA))
    acsT = sum(_dot(piece, triu) for piece in _split3(dAT))
    a_lastT = acsT[:, q - 1:q]
    row_term = acsT - jnp.log(dtT) * LOG2E
    w_rows = dtT * jnp.exp2(a_lastT - acsT)
    d_last = jnp.broadcast_to(jnp.exp2(acs[q - 1:q, :]), (8, acs.shape[1]))
    d_lanes = sum(_dot(piece, expand_ref[...]) for piece in _split3(d_last))[0:1]

    for g in range(SSM_GROUPS):
        gs = slice(g * n, (g + 1) * n)
        cbm_ref[g] = _dot_nt(cb_ref[:, gs], bb_ref[:, gs])
        bT_ref[g] = act_ref[:, b_off + g * n:b_off + (g + 1) * n].T

    left_q = lax.broadcasted_iota(jnp.int32, (q, lanes), 1) < SSM_HEAD_DIM
    left_n = lax.broadcasted_iota(jnp.int32, (n, lanes), 1) < SSM_HEAD_DIM

    def operands(hp):
        g = hp // pairs_per_group
        sl = slice(hp * lanes, (hp + 1) * lanes)
        cbm = cbm_ref[g]
        cg32 = act_ref[:, c_off + g * n:c_off + (g + 1) * n]
        bgT = bT_ref[g]
        tops, btws = [], []
        for h in (2 * hp, 2 * hp + 1):
            a_b = jnp.broadcast_to(acs[:, h:h + 1], (q, n))
            inter = jnp.exp2(a_b) * cg32
            intra = cbm * jnp.exp2(jnp.where(causal, a_b[:, :q] - row_term[h:h + 1, :], -jnp.inf))
            tops.append(jnp.concatenate([inter.astype(BF16), intra.astype(BF16)], axis=1))
            btws.append((bgT * w_rows[h:h + 1, :]).astype(BF16))
        rhs = jnp.concatenate([state_ref[:, sl].astype(BF16), xsb_ref[:, sl]], axis=0)
        return jnp.concatenate(tops, axis=0), jnp.concatenate(btws, axis=0), rhs

    def products(ops):
        lhs_y, lhs_s, rhs = ops
        return _dot(lhs_y, rhs), _dot(lhs_s, rhs[n:])

    def commit(hp, res):
        res_y, res_s = res
        sl = slice(hp * lanes, (hp + 1) * lanes)
        yacc_ref[:, sl] = jnp.where(left_q, res_y[:q], res_y[q:])
        state_ref[:, sl] = state_ref[:, sl] * d_lanes[:, sl] + jnp.where(left_n, res_s[:n], res_s[n:])

    ops = {0: operands(0)}
    for t in range(1, pairs + 1):
        res = products(ops.pop(t - 1))
        if t < pairs:
            ops[t] = operands(t)
        commit(t - 1, res)

    gw = d_inner // SSM_GROUPS
    for g in range(SSM_GROUPS):
        sl = slice(g * gw, (g + 1) * gw)
        zg = z_ref[0, rows, sl]
        yg = (yacc_ref[:, sl] + act_ref[:, sl] * dskip_ref[:, sl]) * _silu_of_twice(zg)
        yg = yg * lax.rsqrt(jnp.mean(yg * yg, axis=-1, keepdims=True) + EPS)
        y_ref[0, rows, sl] = (yg * gnw_ref[:, sl]).astype(y_ref.dtype)


def _ssd_kernel(*refs, q, zero_init):
    xbc_ref, z_ref, dt_ref, dtT_ref = refs[:4]
    h0_ref, c0_ref = (None, None) if zero_init else refs[4:6]
    (convw_ref, convb_ref, dtb_ref, dtbT_ref, alog_ref, alogT_ref, dskip_ref, gnw_ref, expand_ref,
     y_ref, hout_ref, win_ref, *scratch, state_ref) = refs[4 if zero_init else 6:]
    c = pl.program_id(1)
    n = SSM_STATE
    panels, _, lanes = win_ref.shape
    pairs = state_ref.shape[1] // lanes

    @pl.when(c == 0)
    def _():
        if zero_init:
            state_ref[...] = jnp.zeros_like(state_ref)
            win_ref[:, 0:CARRY_ROWS, :] = jnp.zeros((panels, CARRY_ROWS, lanes), F32)
        else:
            for hp in range(pairs):
                state_ref[:, hp * lanes:(hp + 1) * lanes] = (
                    h0_ref[0, 2 * hp:2 * hp + 2].reshape(lanes, n).T)
            for pn in range(panels):
                win_ref[pn, 0:CARRY_ROWS, :] = c0_ref[0, :, pn * lanes:(pn + 1) * lanes]

    for j in range(y_ref.shape[1] // q):
        _ssd_block(j, q, xbc_ref, z_ref, dt_ref, dtT_ref,
                   convw_ref, convb_ref, dtb_ref, dtbT_ref, alog_ref, alogT_ref, dskip_ref,
                   gnw_ref, expand_ref, y_ref, win_ref, *scratch, state_ref)

    @pl.when(c == pl.num_programs(1) - 1)
    def _():
        for hp in range(pairs):
            hout_ref[0, 2 * hp:2 * hp + 2] = (
                state_ref[:, hp * lanes:(hp + 1) * lanes].T.reshape(2, SSM_HEAD_DIM, n))


def _ssd(xbc, z, dt, dtT, state, convw, convb, dtb, dtbT, alog, alogT, dskip, gnw, expand, *, q, nb):
    b, l, conv_dim = xbc.shape
    d_inner = z.shape[2]
    nh = dt.shape[1]
    rows = nb * q
    nc = l // rows
    gn = SSM_GROUPS * SSM_STATE
    lanes = 2 * SSM_HEAD_DIM
    kern = functools.partial(_ssd_kernel, q=q, zero_init=state is None)
    hshape = (d_inner // SSM_HEAD_DIM, SSM_HEAD_DIM, SSM_STATE)
    hspec = pl.BlockSpec((1, *hshape), lambda i, c: (i, 0, 0, 0))
    state_specs = [] if state is None else [
        hspec, pl.BlockSpec((1, CARRY_ROWS, conv_dim), lambda i, c: (i, 0, 0))]
    return pl.pallas_call(
        kern,
        grid=(b, nc),
        in_specs=[pl.BlockSpec((1, rows, conv_dim), lambda i, c: (i, c, 0)),
                  pl.BlockSpec((1, rows, d_inner), lambda i, c: (i, c, 0)),
                  pl.BlockSpec((rows, nh), lambda i, c: (i * nc + c, 0)),
                  pl.BlockSpec((nb, nh, q), lambda i, c: (i * nc + c, 0, 0)),
                  *state_specs,
                  _resident(convw.shape), _resident(convb.shape),
                  _resident(dtb.shape), _resident(dtbT.shape),
                  _resident(alog.shape), _resident(alogT.shape),
                  _resident(dskip.shape), _resident(gnw.shape), _resident(expand.shape)],
        out_specs=[pl.BlockSpec((1, rows, d_inner), lambda i, c: (i, c, 0)), hspec],
        out_shape=[jax.ShapeDtypeStruct((b, l, d_inner), BF16),
                   jax.ShapeDtypeStruct((b, *hshape), F32)],
        scratch_shapes=[pltpu.VMEM((conv_dim // lanes, CARRY_ROWS + q, lanes), F32),
                        pltpu.VMEM((q, conv_dim), F32),
                        pltpu.VMEM((q, d_inner), BF16),
                        pltpu.VMEM((q, gn), BF16),
                        pltpu.VMEM((q, gn), BF16),
                        pltpu.VMEM((SSM_GROUPS, q, q), F32),
                        pltpu.VMEM((SSM_GROUPS, SSM_STATE, q), F32),
                        pltpu.VMEM((q, d_inner), F32),
                        pltpu.VMEM((SSM_STATE, d_inner), F32)],
        compiler_params=_cparams("parallel", "arbitrary"),
        name="ssd_mixer",
    )(xbc, z, dt, dtT, *(state or ()), convw, convb, dtb, dtbT, alog, alogT, dskip, gnw, expand)


def _mix_ffn_kernel(a_ref, wmix_ref, nw_mix_ref, x_ref, nw_in_ref, win_ref, wo_ref,
                    nw_out_ref, o_ref, xn_ref, h_ref, *, hc, parts):
    tm = x_ref.shape[0]
    pr = tm // parts
    hidden = wo_ref.shape[0]
    chunks = list(range(0, hidden, hc))

    def mix(p):
        rows = slice(p * pr, (p + 1) * pr)
        x1 = x_ref[rows, :] + _rms(_dot(a_ref[rows, :], wmix_ref[...]), nw_mix_ref[...])
        o_ref[rows, :] = x1
        xn_ref[rows, :] = _rms(x1, nw_in_ref[...]).astype(BF16)

    def up(p, cs):
        rows = slice(p * pr, (p + 1) * pr)
        for c0 in cs:
            gate = _dot(xn_ref[rows, :], win_ref[:, c0:c0 + hc])
            upv = _dot(xn_ref[rows, :], win_ref[:, hidden + c0:hidden + c0 + hc])
            h_ref[rows, c0:c0 + hc] = (gate * _sigmoid(gate) * upv).astype(BF16)

    def down(p):
        rows = slice(p * pr, (p + 1) * pr)
        y = _dot(h_ref[rows, :], wo_ref[...])
        o_ref[rows, :] = o_ref[rows, :] + _rms(y, nw_out_ref[...])

    mix(0)
    for p in range(parts):
        up(p, chunks[:1])
        if p + 1 < parts:
            mix(p + 1)
        if p >= 1:
            down(p - 1)
        up(p, chunks[1:])
    down(parts - 1)


def _mix_ffn(a, wmix, nw_mix, x, nw_in, win, wo, nw_out, name):
    rows, d = x.shape
    k = a.shape[1]
    hidden = wo.shape[0]
    tm = ROW_TILE
    kern = functools.partial(_mix_ffn_kernel, hc=256, parts=2)
    return pl.pallas_call(
        kern,
        grid=(rows // tm,),
        in_specs=[pl.BlockSpec((tm, k), lambda i: (i, 0)),
                  _resident(wmix.shape), _resident(nw_mix.shape),
                  pl.BlockSpec((tm, d), lambda i: (i, 0)),
                  _resident(nw_in.shape), _resident(win.shape),
                  _resident(wo.shape), _resident(nw_out.shape)],
        out_specs=pl.BlockSpec((tm, d), lambda i: (i, 0)),
        out_shape=jax.ShapeDtypeStruct((rows, d), F32),
        scratch_shapes=[pltpu.VMEM((tm, d), BF16), pltpu.VMEM((tm, hidden), BF16)],
        compiler_params=_cparams("parallel"),
        name=name,
    )(a, wmix, nw_mix, x, nw_in, win, wo, nw_out)


def _qkv_kernel(x_ref, nwq_ref, nwkv_ref, wq_ref, wkv_ref,
                q_ref, k_ref, v_ref, kf_ref, vf_ref):
    x = x_ref[0]
    xh = x * lax.rsqrt(jnp.mean(x * x, axis=-1, keepdims=True) + EPS)
    xq = (xh * nwq_ref[...]).astype(BF16)
    xkv = (xh * nwkv_ref[...]).astype(BF16)
    d = wq_ref.shape[1]
    for c0 in range(0, d, COL_TILE):
        sl = slice(c0, c0 + COL_TILE)
        q_ref[0, :, sl] = (_dot(xq, wq_ref[:, sl]) * Q_SCALE).astype(q_ref.dtype)
        kk = _dot(xkv, wkv_ref[:, sl])
        vv = _dot(xkv, wkv_ref[:, d + c0:d + c0 + COL_TILE])
        k_ref[0, :, sl] = kk.astype(k_ref.dtype)
        v_ref[0, :, sl] = vv.astype(v_ref.dtype)
        kf_ref[0, :, sl] = kk
        vf_ref[0, :, sl] = vv


def _qkv(x, nwq, nwkv, wq, wkv, *, tm, tail_rows):
    b, l, d = x.shape
    nblk = l // tm
    tail_blocks = tail_rows // tm
    rows = lambda bi, i: (bi, i, 0)
    tail = lambda bi, i: (bi, jnp.maximum(i - (nblk - tail_blocks), 0), 0)
    return pl.pallas_call(
        _qkv_kernel,
        grid=(b, nblk),
        in_specs=[pl.BlockSpec((1, tm, d), rows),
                  _resident(nwq.shape), _resident(nwkv.shape),
                  _resident(wq.shape), _resident(wkv.shape)],
        out_specs=[pl.BlockSpec((1, tm, d), rows),
                   pl.BlockSpec((1, tm, d), rows),
                   pl.BlockSpec((1, tm, d), rows),
                   pl.BlockSpec((1, tm, d), tail),
                   pl.BlockSpec((1, tm, d), tail)],
        out_shape=[jax.ShapeDtypeStruct((b, l, d), BF16),
                   jax.ShapeDtypeStruct((b, l, d), BF16),
                   jax.ShapeDtypeStruct((b, l, d), BF16),
                   jax.ShapeDtypeStruct((b, tail_rows, d), F32),
                   jax.ShapeDtypeStruct((b, tail_rows, d), F32)],
        compiler_params=_cparams("parallel", "arbitrary"),
        name="attn_qkv_proj",
    )(x, nwq, nwkv, wq, wkv)


def _bias_kernel(u_ref, o_ref):
    heads, width = u_ref.shape
    for h in range(heads):
        x = jnp.broadcast_to(u_ref[h:h + 1, :] * LOG2E, (CHUNK, width))
        r = pltpu.roll(x, width - CHUNK, 1, stride=1, stride_axis=0)
        o_ref[h] = r[:, :BAND]


def _bias_table(rel_bias):
    heads = rel_bias.shape[0]
    width = BAND + CHUNK
    far = jnp.broadcast_to(rel_bias[:, 2 * REL_CLIP:], (heads, width - (REL_CLIP + CHUNK)))
    near = jnp.flip(rel_bias[:, REL_CLIP - CHUNK + 1:], axis=1)
    u = jnp.concatenate([far, near], axis=1)
    return pl.pallas_call(
        _bias_kernel,
        out_shape=jax.ShapeDtypeStruct((heads, CHUNK, BAND), F32),
        name="attn_rel_bias_table",
    )(u)


def _attn_kernel(*refs, first_chunk, cps, with_cache):
    q_ref, k_ref, v_ref = refs[:3]
    ck_ref, cv_ref = refs[3:5] if with_cache else (None, None)
    bias_ref, o_ref, kbuf_ref, vbuf_ref = refs[5 if with_cache else 3:]
    step = pl.program_id(1)

    @pl.when(step == 0)
    def _():
        r = CHUNK + (ck_ref.shape[1] if with_cache else 0)
        kbuf_ref[0:CHUNK, :] = jnp.zeros((CHUNK, kbuf_ref.shape[1]), BF16)
        vbuf_ref[0:CHUNK, :] = jnp.zeros((CHUNK, vbuf_ref.shape[1]), BF16)
        if with_cache:
            kbuf_ref[CHUNK:r, :] = ck_ref[0].astype(BF16)
            vbuf_ref[CHUNK:r, :] = cv_ref[0].astype(BF16)
        kbuf_ref[r:, :] = k_ref[0]
        vbuf_ref[r:, :] = v_ref[0]

    pairs = bias_ref.shape[0]
    pw = 2 * ATT_HEAD_DIM
    left = lax.broadcasted_iota(jnp.int32, (CHUNK, pw), 1) < ATT_HEAD_DIM
    units = [(ci, hp) for ci in range(cps) for hp in range(pairs)]

    def body(static_step):
        def geometry(u):
            if static_step is None:
                return BAND_CHUNKS + 1, False
            valid = min(static_step * cps + units[u][0] + first_chunk + 1, BAND_CHUNKS + 1)
            return valid + (1 - valid % 2), valid % 2 == 0

        def band(ref, u):
            ci, hp = units[u]
            take = geometry(u)[0] * CHUNK
            if static_step is None:
                first = (step * cps + ci + first_chunk - BAND_CHUNKS + 1) * CHUNK
                rows = pl.ds(pl.multiple_of(first, CHUNK), take)
            else:
                end = (static_step * cps + ci + first_chunk + 2) * CHUNK
                rows = slice(end - take, end)
            return ref[rows, hp * pw:(hp + 1) * pw]

        def scores(u):
            ci, hp = units[u]
            chunks, masked = geometry(u)
            qp = q_ref[0, ci * CHUNK:(ci + 1) * CHUNK, hp * pw:(hp + 1) * pw]
            zero = jnp.zeros_like(qp)
            q2 = jnp.concatenate([jnp.where(left, qp, zero), jnp.where(left, zero, qp)], axis=0)
            s = _dot_nt(q2, band(kbuf_ref, u)) + bias_ref[hp, :, BAND - chunks * CHUNK:]
            if masked:
                lane = lax.broadcasted_iota(jnp.int32, s.shape, 1)
                s = jnp.where(lane >= CHUNK, s, -jnp.inf)
            return s

        def softmax(s):
            pr = jnp.exp2(s - jnp.max(s, axis=-1, keepdims=True))
            return pr.astype(BF16), jnp.sum(pr, axis=-1, keepdims=True)

        def weighted(u, pr):
            v = band(vbuf_ref, u)
            past = (geometry(u)[0] - 1) * CHUNK
            own = _dot(pr[:, past:], v[past:])
            return own if past == 0 else _dot(pr[:, :past], v[:past]) + own

        def finish(u, r, denom):
            ci, hp = units[u]
            r = r / denom
            o_ref[0, ci * CHUNK:(ci + 1) * CHUNK, hp * pw:(hp + 1) * pw] = (
                jnp.where(left, r[:CHUNK], r[CHUNK:]).astype(o_ref.dtype))

        n = len(units)
        s, pr, den, acc = {}, {}, {}, {}
        for t in range(n + 3):
            if t < n:
                s[t] = scores(t)
            if 0 <= t - 2 < n:
                acc[t - 2] = weighted(t - 2, pr.pop(t - 2))
            if 0 <= t - 1 < n:
                pr[t - 1], den[t - 1] = softmax(s.pop(t - 1))
            if 0 <= t - 3 < n:
                finish(t - 3, acc.pop(t - 3), den.pop(t - 3))

    short_steps = max(0, BAND_CHUNKS - first_chunk) // cps
    for st in range(short_steps):
        pl.when(step == st)(functools.partial(body, st))
    pl.when(step >= short_steps)(functools.partial(body, None))


def _attention(q, k, v, bias, *, first_chunk, cps, cache=None):
    b, l, d = q.shape
    lk = k.shape[1]
    assert BAND_CHUNKS % cps == 0 and l % (cps * CHUNK) == 0
    kern = functools.partial(_attn_kernel, first_chunk=first_chunk, cps=cps,
                             with_cache=cache is not None)
    cache = list(cache or ())
    seq = lambda rows: pl.BlockSpec((1, rows, d), lambda i, c: (i, 0, 0))
    band_rows = CHUNK + lk + sum(a.shape[1] for a in cache[:1])
    return pl.pallas_call(
        kern,
        grid=(b, l // (cps * CHUNK)),
        in_specs=[pl.BlockSpec((1, cps * CHUNK, d), lambda i, c: (i, c, 0)),
                  seq(lk), seq(lk), *[seq(a.shape[1]) for a in cache],
                  _resident(bias.shape)],
        out_specs=pl.BlockSpec((1, cps * CHUNK, d), lambda i, c: (i, c, 0)),
        out_shape=jax.ShapeDtypeStruct((b, l, d), BF16),
        scratch_shapes=[pltpu.VMEM((band_rows, d), BF16), pltpu.VMEM((band_rows, d), BF16)],
        compiler_params=_cparams("parallel", "arbitrary"),
        name="band_attention",
    )(q, k, v, *cache, bias)


def _row(v):
    return v.reshape(1, -1).astype(F32)


def _ssd_layer(x3, state, p, nw, *, q, nb):
    b, l, d = x3.shape
    rows = b * l
    d_inner = p["wout"].shape[0]
    conv_dim = p["convw"].shape[1]
    x = x3.reshape(rows, d)
    z, xbc, dt, dtT = _inproj(x, _row(nw[0]), p["win"], d_inner=d_inner, conv_dim=conv_dim,
                              nh=p["dtb"].shape[1])
    xbc3 = xbc.reshape(b, l, conv_dim)
    if state is not None:
        h0, conv_prev = state
        state = (h0, jnp.pad(conv_prev, ((0, 0), (CARRY_ROWS - (CONV_W - 1), 0), (0, 0))))
    dtT = jnp.transpose(dtT.reshape(-1, rows // q, q), (1, 0, 2))
    y, h_new = _ssd(xbc3, z.reshape(b, l, d_inner), dt, dtT, state,
                 p["convw"], p["convb"], p["dtb"], p["dtbT"], p["alog"], p["alogT"],
                 p["dskip"], p["gnw"], p["expand"], q=q, nb=nb)
    new_conv = xbc3[:, l - (CONV_W - 1):]
    return y.reshape(rows, d_inner), h_new, new_conv


def _layer_tail(mixed, w_mix, x3, fp, nw, name):
    b, l, d = x3.shape
    out = _mix_ffn(mixed, w_mix, _row(nw[1]), x3.reshape(b * l, d), _row(nw[2]),
                   fp["win"], fp["wo"], _row(nw[3]), name)
    return out.reshape(b, l, d)


def kernel(x_prompt, x_sample, state_ssm, state_conv, cache_k, cache_v, norm_w,
           ssm_w_in, ssm_conv_w, ssm_conv_b, ssm_dt_bias, ssm_A_log, ssm_D, ssm_norm_w, ssm_w_out,
           kv_norm_w, w_kv, attn_w_q, attn_rel_bias, attn_w_o, ffn_w_in, ffn_w_out):
    bp, lp, d = x_prompt.shape
    bs, ls, _ = x_sample.shape
    n_a = ssm_w_in.shape[0]
    depth = norm_w.shape[0]
    d_inner = ssm_w_out.shape[1]
    heads = ssm_dt_bias.shape[1]
    conv_dim = ssm_conv_w.shape[2]
    att_dim = attn_w_q.shape[2]
    att_heads = att_dim // ATT_HEAD_DIM
    hidden = ffn_w_out.shape[1]

    xp, xs = x_prompt, x_sample
    ssm_p, conv_p, ssm_s, conv_s = [], [], [], []
    outs_kv = None
    for layer in range(depth):
        nw = norm_w[layer]
        fp = {"win": ffn_w_in[layer].astype(BF16), "wo": ffn_w_out[layer].astype(BF16)}
        if layer < n_a:
            a = layer
            sp = {"win": jnp.pad(ssm_w_in[a], ((0, 0), (0, 2 * SSM_HEAD_DIM - heads))).astype(BF16),
                  "convw": 0.5 * ssm_conv_w[a], "convb": _row(0.5 * ssm_conv_b[a]),
                  "dtb": _row(ssm_dt_bias[a]), "dtbT": ssm_dt_bias[a].reshape(-1, 1),
                  "alog": _row(ssm_A_log[a]), "alogT": ssm_A_log[a].reshape(-1, 1),
                  "dskip": _row(jnp.repeat(ssm_D[a], SSM_HEAD_DIM)),
                  "gnw": _row(ssm_norm_w[a]), "wout": ssm_w_out[a].astype(BF16),
                  "expand": jnp.repeat(jnp.eye(heads, dtype=BF16), SSM_HEAD_DIM, axis=1)}
            mp, hp_new, cp_new = _ssd_layer(xp, None, sp, nw, q=SSD_BLOCK, nb=SSD_BLOCKS_PER_STEP)
            ms, hs_new, cs_new = _ssd_layer(xs, (state_ssm[a], state_conv[a]), sp, nw, q=ls, nb=1)
            w_mix = sp["wout"]
            ssm_p.append(hp_new)
            conv_p.append(cp_new)
            ssm_s.append(hs_new)
            conv_s.append(cs_new)
        else:
            i = layer - n_a
            wq = attn_w_q[i].astype(BF16)
            wkv = w_kv.astype(BF16)
            w_mix = attn_w_o[i].astype(BF16)
            bias = _bias_table(attn_rel_bias[i]).reshape(att_heads // 2, 2 * CHUNK, BAND)
            rows_p = min(BAND_CHUNKS * CHUNK, lp)
            qp, kpb, vpb, kpf, vpf = _qkv(xp, _row(nw[0]), _row(kv_norm_w), wq, wkv,
                                          tm=ROW_TILE, tail_rows=rows_p)
            mp = _attention(qp, kpb, vpb, bias, first_chunk=0,
                            cps=ATTN_CHUNKS_PER_STEP).reshape(bp * lp, att_dim)
            flat = lambda a: a.reshape(bs, ls, att_dim)
            qs, ksb, vsb, ksf, vsf = map(flat, _qkv(xs.reshape(1, bs * ls, d), _row(nw[0]),
                                                    _row(kv_norm_w), wq, wkv, tm=ROW_TILE,
                                                    tail_rows=bs * ls))
            r = cache_k.shape[1]
            cache = (cache_k.astype(BF16).reshape(bs, r, att_dim),
                     cache_v.astype(BF16).reshape(bs, r, att_dim))
            ms = _attention(qs, ksb, vsb, bias, first_chunk=BAND_CHUNKS, cps=1,
                            cache=cache).reshape(bs * ls, att_dim)
            if outs_kv is None:
                outs_kv = (kpf.reshape(bp, rows_p, att_heads, ATT_HEAD_DIM),
                           vpf.reshape(bp, rows_p, att_heads, ATT_HEAD_DIM),
                           ksf.reshape(bs, ls, att_heads, ATT_HEAD_DIM),
                           vsf.reshape(bs, ls, att_heads, ATT_HEAD_DIM))
        xp = _layer_tail(mp, w_mix, xp, fp, nw, "mix_ffn_prompt")
        xs = _layer_tail(ms, w_mix, xs, fp, nw, "mix_ffn_sample")
    kp_out, vp_out, ks_out, vs_out = outs_kv
    return (xp, xs, jnp.stack(ssm_p), jnp.stack(conv_p), kp_out, vp_out,
            jnp.stack(ssm_s), jnp.stack(conv_s), ks_out, vs_out)
```

```python
import functools

import jax
import jax.numpy as jnp
from jax import lax
from jax.experimental import pallas as pl
from jax.experimental.pallas import tpu as pltpu

F32 = jnp.float32
BF16 = jnp.bfloat16

EPS = 1e-6
CHUNK = 64
BAND_CHUNKS = 8
BAND = (BAND_CHUNKS + 1) * CHUNK
REL_CLIP = 256
SSM_HEAD_DIM = 64
SSM_GROUPS = 4
SSM_STATE = 128
CONV_W = 4
ATT_HEAD_DIM = 64
LOG2E = 1.4426950408889634
Q_SCALE = ATT_HEAD_DIM ** -0.5 * LOG2E
CARRY_ROWS = 8

VMEM_LIMIT_BYTES = 56 * 1024 * 1024
ROW_TILE = 512
COL_TILE = 512
ATTN_CHUNKS_PER_STEP = 8
SSD_BLOCK = 128
SSD_BLOCKS_PER_STEP = 4


def _cparams(*sem):
    return pltpu.CompilerParams(dimension_semantics=sem,
                                vmem_limit_bytes=VMEM_LIMIT_BYTES)


def _resident(shape):
    zeros = (0,) * len(shape)
    return pl.BlockSpec(shape, lambda *_: zeros, pipeline_mode=pl.Buffered(1))


def _rms(x, w):
    ms = jnp.mean(x * x, axis=-1, keepdims=True)
    return x * lax.rsqrt(ms + EPS) * w


def _sigmoid(x):
    return 1.0 / (1.0 + jnp.exp2(x * -LOG2E))


def _silu_of_twice(h):
    return h + h * jnp.tanh(h)


def _softplus(x):
    return jnp.maximum(x, 0.0) + jnp.log(1.0 + jnp.exp(-jnp.abs(x)))


def _dot(a, b):
    return jnp.dot(a, b, preferred_element_type=F32)


def _dot_nt(a, b):
    return lax.dot_general(a, b, (((1,), (1,)), ((), ())), preferred_element_type=F32)


def _split3(x):
    hi = x.astype(BF16)
    r1 = x - hi.astype(F32)
    mid = r1.astype(BF16)
    lo = (r1 - mid.astype(F32)).astype(BF16)
    return hi, mid, lo


def _inproj_kernel(x_ref, nw_ref, w_ref, z_ref, xbc_ref, dt_ref, dtT_ref):
    xn = _rms(x_ref[...], nw_ref[...]).astype(BF16)
    d_inner, conv_dim, nh = z_ref.shape[1], xbc_ref.shape[1], dt_ref.shape[1]
    for c0 in range(0, d_inner, COL_TILE):
        z_ref[:, c0:c0 + COL_TILE] = 0.5 * _dot(xn, w_ref[:, c0:c0 + COL_TILE])
    for c0 in range(0, conv_dim, COL_TILE):
        xbc_ref[:, c0:c0 + COL_TILE] = _dot(xn, w_ref[:, d_inner + c0:d_inner + c0 + COL_TILE])
    dt = _dot(xn, w_ref[:, d_inner + conv_dim:])
    dt_ref[...] = dt[:, :nh]
    dtT_ref[...] = dt.T[:nh, :]


def _inproj(x, nw, w, *, d_inner, conv_dim, nh):
    rows, d = x.shape
    tm = ROW_TILE
    return pl.pallas_call(
        _inproj_kernel,
        grid=(rows // tm,),
        in_specs=[pl.BlockSpec((tm, d), lambda i: (i, 0)),
                  _resident(nw.shape), _resident(w.shape)],
        out_specs=[pl.BlockSpec((tm, d_inner), lambda i: (i, 0)),
                   pl.BlockSpec((tm, conv_dim), lambda i: (i, 0)),
                   pl.BlockSpec((tm, nh), lambda i: (i, 0)),
                   pl.BlockSpec((nh, tm), lambda i: (0, i))],
        out_shape=[jax.ShapeDtypeStruct((rows, d_inner), F32),
                   jax.ShapeDtypeStruct((rows, conv_dim), F32),
                   jax.ShapeDtypeStruct((rows, nh), F32),
                   jax.ShapeDtypeStruct((nh, rows), F32)],
        compiler_params=_cparams("parallel"),
        name="ssm_in_proj",
    )(x, nw, w)


def _ssd_block(j, q, xbc_ref, z_ref, dt_ref, dtT_ref,
               convw_ref, convb_ref, dtb_ref, dtbT_ref, alog_ref, alogT_ref, dskip_ref, gnw_ref,
               expand_ref, y_ref,
               win_ref, act_ref, xsb_ref, bb_ref, cb_ref, cbm_ref, bT_ref, yacc_ref, state_ref):
    rows = slice(j * q, (j + 1) * q)
    d_inner = y_ref.shape[2]
    n = SSM_STATE
    gn = SSM_GROUPS * n
    lanes = 2 * SSM_HEAD_DIM
    pairs = d_inner // lanes
    pairs_per_group = pairs // SSM_GROUPS
    panels = win_ref.shape[0]
    b_off, c_off = d_inner, d_inner + gn

    for pn in range(panels):
        sl = slice(pn * lanes, (pn + 1) * lanes)
        win_ref[pn, CARRY_ROWS:CARRY_ROWS + q, :] = xbc_ref[0, rows, sl]
        conv = convb_ref[:, sl]
        for k in range(CONV_W):
            r0 = CARRY_ROWS - (CONV_W - 1) + k
            conv = conv + win_ref[pn, r0:r0 + q, :] * convw_ref[k:k + 1, sl]
        win_ref[pn, 0:CARRY_ROWS, :] = win_ref[pn, q:q + CARRY_ROWS, :]
        act = _silu_of_twice(conv)
        act_ref[:, sl] = act
        if pn * lanes < b_off:
            xsb_ref[:, sl] = act.astype(BF16)
        elif pn * lanes < c_off:
            bb_ref[:, pn * lanes - b_off:(pn + 1) * lanes - b_off] = act.astype(BF16)
        else:
            cb_ref[:, pn * lanes - c_off:(pn + 1) * lanes - c_off] = act.astype(BF16)

    dt = _softplus(dt_ref[rows, :] + dtb_ref[...])
    dtT = _softplus(dtT_ref[j] + dtbT_ref[...])
    dA = dt * (-LOG2E * jnp.exp(alog_ref[...]))
    dAT = dtT * (-LOG2E * jnp.exp(alogT_ref[...]))
    row = lax.broadcasted_iota(jnp.int32, (q, q), 0)
    col = lax.broadcasted_iota(jnp.int32, (q, q), 1)
    causal = row >= col
    tril = jnp.where(causal, 1.0, 0.0).astype(BF16)
    triu = jnp.where(row <= col, 1.0, 0.0).astype(BF16)
    acs = sum(_dot(tril, piece) for piece in _split3(dA))
    acsT = sum(_dot(piece, triu) for piece in _split3(dAT))
    a_lastT = acsT[:, q - 1:q]
    row_term = acsT - jnp.log(dtT) * LOG2E
    w_rows = dtT * jnp.exp2(a_lastT - acsT)
    d_last = jnp.broadcast_to(jnp.exp2(acs[q - 1:q, :]), (8, acs.shape[1]))
    d_lanes = sum(_dot(piece, expand_ref[...]) for piece in _split3(d_last))[0:1]

    for g in range(SSM_GROUPS):
        gs = slice(g * n, (g + 1) * n)
        cbm_ref[g] = _dot_nt(cb_ref[:, gs], bb_ref[:, gs])
        bT_ref[g] = act_ref[:, b_off + g * n:b_off + (g + 1) * n].T

    left_q = lax.broadcasted_iota(jnp.int32, (q, lanes), 1) < SSM_HEAD_DIM
    left_n = lax.broadcasted_iota(jnp.int32, (n, lanes), 1) < SSM_HEAD_DIM

    def operands(hp):
        g = hp // pairs_per_group
        sl = slice(hp * lanes, (hp + 1) * lanes)
        cbm = cbm_ref[g]
        cg32 = act_ref[:, c_off + g * n:c_off + (g + 1) * n]
        bgT = bT_ref[g]
        tops, btws = [], []
        for h in (2 * hp, 2 * hp + 1):
            a_b = jnp.broadcast_to(acs[:, h:h + 1], (q, n))
            inter = jnp.exp2(a_b) * cg32
            intra = cbm * jnp.exp2(jnp.where(causal, a_b[:, :q] - row_term[h:h + 1, :], -jnp.inf))
            tops.append(jnp.concatenate([inter.astype(BF16), intra.astype(BF16)], axis=1))
            btws.append((bgT * w_rows[h:h + 1, :]).astype(BF16))
        rhs = jnp.concatenate([state_ref[:, sl].astype(BF16), xsb_ref[:, sl]], axis=0)
        return jnp.concatenate(tops, axis=0), jnp.concatenate(btws, axis=0), rhs

    def products(ops):
        lhs_y, lhs_s, rhs = ops
        return _dot(lhs_y, rhs), _dot(lhs_s, rhs[n:])

    def commit(hp, res):
        res_y, res_s = res
        sl = slice(hp * lanes, (hp + 1) * lanes)
        yacc_ref[:, sl] = jnp.where(left_q, res_y[:q], res_y[q:])
        state_ref[:, sl] = state_ref[:, sl] * d_lanes[:, sl] + jnp.where(left_n, res_s[:n], res_s[n:])

    ops = {0: operands(0)}
    for t in range(1, pairs + 1):
        res = products(ops.pop(t - 1))
        if t < pairs:
            ops[t] = operands(t)
        commit(t - 1, res)

    gw = d_inner // SSM_GROUPS
    for g in range(SSM_GROUPS):
        sl = slice(g * gw, (g + 1) * gw)
        zg = z_ref[0, rows, sl]
        yg = (yacc_ref[:, sl] + act_ref[:, sl] * dskip_ref[:, sl]) * _silu_of_twice(zg)
        yg = yg * lax.rsqrt(jnp.mean(yg * yg, axis=-1, keepdims=True) + EPS)
        y_ref[0, rows, sl] = (yg * gnw_ref[:, sl]).astype(y_ref.dtype)


def _ssd_kernel(*refs, q, zero_init):
    xbc_ref, z_ref, dt_ref, dtT_ref = refs[:4]
    h0_ref, c0_ref = (None, None) if zero_init else refs[4:6]
    (convw_ref, convb_ref, dtb_ref, dtbT_ref, alog_ref, alogT_ref, dskip_ref, gnw_ref, expand_ref,
     y_ref, hout_ref, win_ref, *scratch, state_ref) = refs[4 if zero_init else 6:]
    c = pl.program_id(1)
    n = SSM_STATE
    panels, _, lanes = win_ref.shape
    pairs = state_ref.shape[1] // lanes

    @pl.when(c == 0)
    def _():
        if zero_init:
            state_ref[...] = jnp.zeros_like(state_ref)
            win_ref[:, 0:CARRY_ROWS, :] = jnp.zeros((panels, CARRY_ROWS, lanes), F32)
        else:
            for hp in range(pairs):
                state_ref[:, hp * lanes:(hp + 1) * lanes] = (
                    h0_ref[0, 2 * hp:2 * hp + 2].reshape(lanes, n).T)
            for pn in range(panels):
                win_ref[pn, 0:CARRY_ROWS, :] = c0_ref[0, :, pn * lanes:(pn + 1) * lanes]

    for j in range(y_ref.shape[1] // q):
        _ssd_block(j, q, xbc_ref, z_ref, dt_ref, dtT_ref,
                   convw_ref, convb_ref, dtb_ref, dtbT_ref, alog_ref, alogT_ref, dskip_ref,
                   gnw_ref, expand_ref, y_ref, win_ref, *scratch, state_ref)

    @pl.when(c == pl.num_programs(1) - 1)
    def _():
        for hp in range(pairs):
            hout_ref[0, 2 * hp:2 * hp + 2] = (
                state_ref[:, hp * lanes:(hp + 1) * lanes].T.reshape(2, SSM_HEAD_DIM, n))


def _ssd(xbc, z, dt, dtT, state, convw, convb, dtb, dtbT, alog, alogT, dskip, gnw, expand, *, q, nb):
    b, l, conv_dim = xbc.shape
    d_inner = z.shape[2]
    nh = dt.shape[1]
    rows = nb * q
    nc = l // rows
    gn = SSM_GROUPS * SSM_STATE
    lanes = 2 * SSM_HEAD_DIM
    kern = functools.partial(_ssd_kernel, q=q, zero_init=state is None)
    hshape = (d_inner // SSM_HEAD_DIM, SSM_HEAD_DIM, SSM_STATE)
    hspec = pl.BlockSpec((1, *hshape), lambda i, c: (i, 0, 0, 0))
    state_specs = [] if state is None else [
        hspec, pl.BlockSpec((1, CARRY_ROWS, conv_dim), lambda i, c: (i, 0, 0))]
    return pl.pallas_call(
        kern,
        grid=(b, nc),
        in_specs=[pl.BlockSpec((1, rows, conv_dim), lambda i, c: (i, c, 0)),
                  pl.BlockSpec((1, rows, d_inner), lambda i, c: (i, c, 0)),
                  pl.BlockSpec((rows, nh), lambda i, c: (i * nc + c, 0)),
                  pl.BlockSpec((nb, nh, q), lambda i, c: (i * nc + c, 0, 0)),
                  *state_specs,
                  _resident(convw.shape), _resident(convb.shape),
                  _resident(dtb.shape), _resident(dtbT.shape),
                  _resident(alog.shape), _resident(alogT.shape),
                  _resident(dskip.shape), _resident(gnw.shape), _resident(expand.shape)],
        out_specs=[pl.BlockSpec((1, rows, d_inner), lambda i, c: (i, c, 0)), hspec],
        out_shape=[jax.ShapeDtypeStruct((b, l, d_inner), BF16),
                   jax.ShapeDtypeStruct((b, *hshape), F32)],
        scratch_shapes=[pltpu.VMEM((conv_dim // lanes, CARRY_ROWS + q, lanes), F32),
                        pltpu.VMEM((q, conv_dim), F32),
                        pltpu.VMEM((q, d_inner), BF16),
                        pltpu.VMEM((q, gn), BF16),
                        pltpu.VMEM((q, gn), BF16),
                        pltpu.VMEM((SSM_GROUPS, q, q), F32),
                        pltpu.VMEM((SSM_GROUPS, SSM_STATE, q), F32),
                        pltpu.VMEM((q, d_inner), F32),
                        pltpu.VMEM((SSM_STATE, d_inner), F32)],
        compiler_params=_cparams("parallel", "arbitrary"),
        name="ssd_mixer",
    )(xbc, z, dt, dtT, *(state or ()), convw, convb, dtb, dtbT, alog, alogT, dskip, gnw, expand)


def _mix_ffn_kernel(a_ref, wmix_ref, nw_mix_ref, x_ref, nw_in_ref, win_ref, wo_ref,
                    nw_out_ref, o_ref, xn_ref, h_ref, *, hc, parts):
    tm = x_ref.shape[0]
    pr = tm // parts
    hidden = wo_ref.shape[0]
    chunks = list(range(0, hidden, hc))

    def mix(p):
        rows = slice(p * pr, (p + 1) * pr)
        x1 = x_ref[rows, :] + _rms(_dot(a_ref[rows, :], wmix_ref[...]), nw_mix_ref[...])
        o_ref[rows, :] = x1
        xn_ref[rows, :] = _rms(x1, nw_in_ref[...]).astype(BF16)

    def up(p, cs):
        rows = slice(p * pr, (p + 1) * pr)
        for c0 in cs:
            gate = _dot(xn_ref[rows, :], win_ref[:, c0:c0 + hc])
            upv = _dot(xn_ref[rows, :], win_ref[:, hidden + c0:hidden + c0 + hc])
            h_ref[rows, c0:c0 + hc] = (gate * _sigmoid(gate) * upv).astype(BF16)

    def down(p):
        rows = slice(p * pr, (p + 1) * pr)
        y = _dot(h_ref[rows, :], wo_ref[...])
        o_ref[rows, :] = o_ref[rows, :] + _rms(y, nw_out_ref[...])

    mix(0)
    for p in range(parts):
        up(p, chunks[:1])
        if p + 1 < parts:
            mix(p + 1)
        if p >= 1:
            down(p - 1)
        up(p, chunks[1:])
    down(parts - 1)


def _mix_ffn(a, wmix, nw_mix, x, nw_in, win, wo, nw_out, name):
    rows, d = x.shape
    k = a.shape[1]
    hidden = wo.shape[0]
    tm = ROW_TILE
    kern = functools.partial(_mix_ffn_kernel, hc=256, parts=2)
    return pl.pallas_call(
        kern,
        grid=(rows // tm,),
        in_specs=[pl.BlockSpec((tm, k), lambda i: (i, 0)),
                  _resident(wmix.shape), _resident(nw_mix.shape),
                  pl.BlockSpec((tm, d), lambda i: (i, 0)),
                  _resident(nw_in.shape), _resident(win.shape),
                  _resident(wo.shape), _resident(nw_out.shape)],
        out_specs=pl.BlockSpec((tm, d), lambda i: (i, 0)),
        out_shape=jax.ShapeDtypeStruct((rows, d), F32),
        scratch_shapes=[pltpu.VMEM((tm, d), BF16), pltpu.VMEM((tm, hidden), BF16)],
        compiler_params=_cparams("parallel"),
        name=name,
    )(a, wmix, nw_mix, x, nw_in, win, wo, nw_out)


def _qkv_kernel(x_ref, nwq_ref, nwkv_ref, wq_ref, wkv_ref,
                q_ref, k_ref, v_ref, kf_ref, vf_ref):
    x = x_ref[0]
    xh = x * lax.rsqrt(jnp.mean(x * x, axis=-1, keepdims=True) + EPS)
    xq = (xh * nwq_ref[...]).astype(BF16)
    xkv = (xh * nwkv_ref[...]).astype(BF16)
    d = wq_ref.shape[1]
    for c0 in range(0, d, COL_TILE):
        sl = slice(c0, c0 + COL_TILE)
        q_ref[0, :, sl] = (_dot(xq, wq_ref[:, sl]) * Q_SCALE).astype(q_ref.dtype)
        kk = _dot(xkv, wkv_ref[:, sl])
        vv = _dot(xkv, wkv_ref[:, d + c0:d + c0 + COL_TILE])
        k_ref[0, :, sl] = kk.astype(k_ref.dtype)
        v_ref[0, :, sl] = vv.astype(v_ref.dtype)
        kf_ref[0, :, sl] = kk
        vf_ref[0, :, sl] = vv


def _qkv(x, nwq, nwkv, wq, wkv, *, tm, tail_rows):
    b, l, d = x.shape
    nblk = l // tm
    tail_blocks = tail_rows // tm
    rows = lambda bi, i: (bi, i, 0)
    tail = lambda bi, i: (bi, jnp.maximum(i - (nblk - tail_blocks), 0), 0)
    return pl.pallas_call(
        _qkv_kernel,
        grid=(b, nblk),
        in_specs=[pl.BlockSpec((1, tm, d), rows),
                  _resident(nwq.shape), _resident(nwkv.shape),
                  _resident(wq.shape), _resident(wkv.shape)],
        out_specs=[pl.BlockSpec((1, tm, d), rows),
                   pl.BlockSpec((1, tm, d), rows),
                   pl.BlockSpec((1, tm, d), rows),
                   pl.BlockSpec((1, tm, d), tail),
                   pl.BlockSpec((1, tm, d), tail)],
        out_shape=[jax.ShapeDtypeStruct((b, l, d), BF16),
                   jax.ShapeDtypeStruct((b, l, d), BF16),
                   jax.ShapeDtypeStruct((b, l, d), BF16),
                   jax.ShapeDtypeStruct((b, tail_rows, d), F32),
                   jax.ShapeDtypeStruct((b, tail_rows, d), F32)],
        compiler_params=_cparams("parallel", "arbitrary"),
        name="attn_qkv_proj",
    )(x, nwq, nwkv, wq, wkv)


def _bias_kernel(u_ref, o_ref):
    heads, width = u_ref.shape
    for h in range(heads):
        x = jnp.broadcast_to(u_ref[h:h + 1, :] * LOG2E, (CHUNK, width))
        r = pltpu.roll(x, width - CHUNK, 1, stride=1, stride_axis=0)
        o_ref[h] = r[:, :BAND]


def _bias_table(rel_bias):
    heads = rel_bias.shape[0]
    width = BAND + CHUNK
    far = jnp.broadcast_to(rel_bias[:, 2 * REL_CLIP:], (heads, width - (REL_CLIP + CHUNK)))
    near = jnp.flip(rel_bias[:, REL_CLIP - CHUNK + 1:], axis=1)
    u = jnp.concatenate([far, near], axis=1)
    return pl.pallas_call(
        _bias_kernel,
        out_shape=jax.ShapeDtypeStruct((heads, CHUNK, BAND), F32),
        name="attn_rel_bias_table",
    )(u)


def _attn_kernel(*refs, first_chunk, cps, with_cache):
    q_ref, k_ref, v_ref = refs[:3]
    ck_ref, cv_ref = refs[3:5] if with_cache else (None, None)
    bias_ref, o_ref, kbuf_ref, vbuf_ref = refs[5 if with_cache else 3:]
    step = pl.program_id(1)

    @pl.when(step == 0)
    def _():
        r = CHUNK + (ck_ref.shape[1] if with_cache else 0)
        kbuf_ref[0:CHUNK, :] = jnp.zeros((CHUNK, kbuf_ref.shape[1]), BF16)
        vbuf_ref[0:CHUNK, :] = jnp.zeros((CHUNK, vbuf_ref.shape[1]), BF16)
        if with_cache:
            kbuf_ref[CHUNK:r, :] = ck_ref[0].astype(BF16)
            vbuf_ref[CHUNK:r, :] = cv_ref[0].astype(BF16)
        kbuf_ref[r:, :] = k_ref[0]
        vbuf_ref[r:, :] = v_ref[0]

    pairs = bias_ref.shape[0]
    pw = 2 * ATT_HEAD_DIM
    left = lax.broadcasted_iota(jnp.int32, (CHUNK, pw), 1) < ATT_HEAD_DIM
    units = [(ci, hp) for ci in range(cps) for hp in range(pairs)]

    def body(static_step):
        def geometry(u):
            if static_step is None:
                return BAND_CHUNKS + 1, False
            valid = min(static_step * cps + units[u][0] + first_chunk + 1, BAND_CHUNKS + 1)
            return valid + (1 - valid % 2), valid % 2 == 0

        def band(ref, u):
            ci, hp = units[u]
            take = geometry(u)[0] * CHUNK
            if static_step is None:
                first = (step * cps + ci + first_chunk - BAND_CHUNKS + 1) * CHUNK
                rows = pl.ds(pl.multiple_of(first, CHUNK), take)
            else:
                end = (static_step * cps + ci + first_chunk + 2) * CHUNK
                rows = slice(end - take, end)
            return ref[rows, hp * pw:(hp + 1) * pw]

        def scores(u):
            ci, hp = units[u]
            chunks, masked = geometry(u)
            qp = q_ref[0, ci * CHUNK:(ci + 1) * CHUNK, hp * pw:(hp + 1) * pw]
            zero = jnp.zeros_like(qp)
            q2 = jnp.concatenate([jnp.where(left, qp, zero), jnp.where(left, zero, qp)], axis=0)
            s = _dot_nt(q2, band(kbuf_ref, u)) + bias_ref[hp, :, BAND - chunks * CHUNK:]
            if masked:
                lane = lax.broadcasted_iota(jnp.int32, s.shape, 1)
                s = jnp.where(lane >= CHUNK, s, -jnp.inf)
            return s

        def softmax(s):
            pr = jnp.exp2(s - jnp.max(s, axis=-1, keepdims=True))
            return pr.astype(BF16), jnp.sum(pr, axis=-1, keepdims=True)

        def weighted(u, pr):
            v = band(vbuf_ref, u)
            past = (geometry(u)[0] - 1) * CHUNK
            own = _dot(pr[:, past:], v[past:])
            return own if past == 0 else _dot(pr[:, :past], v[:past]) + own

        def finish(u, r, denom):
            ci, hp = units[u]
            r = r / denom
            o_ref[0, ci * CHUNK:(ci + 1) * CHUNK, hp * pw:(hp + 1) * pw] = (
                jnp.where(left, r[:CHUNK], r[CHUNK:]).astype(o_ref.dtype))

        n = len(units)
        s, pr, den, acc = {}, {}, {}, {}
        for t in range(n + 3):
            if t < n:
                s[t] = scores(t)
            if 0 <= t - 2 < n:
                acc[t - 2] = weighted(t - 2, pr.pop(t - 2))
            if 0 <= t - 1 < n:
                pr[t - 1], den[t - 1] = softmax(s.pop(t - 1))
            if 0 <= t - 3 < n:
                finish(t - 3, acc.pop(t - 3), den.pop(t - 3))

    short_steps = max(0, BAND_CHUNKS - first_chunk) // cps
    for st in range(short_steps):
        pl.when(step == st)(functools.partial(body, st))
    pl.when(step >= short_steps)(functools.partial(body, None))


def _attention(q, k, v, bias, *, first_chunk, cps, cache=None):
    b, l, d = q.shape
    lk = k.shape[1]
    assert BAND_CHUNKS % cps == 0 and l % (cps * CHUNK) == 0
    kern = functools.partial(_attn_kernel, first_chunk=first_chunk, cps=cps,
                             with_cache=cache is not None)
    cache = list(cache or ())
    seq = lambda rows: pl.BlockSpec((1, rows, d), lambda i, c: (i, 0, 0))
    band_rows = CHUNK + lk + sum(a.shape[1] for a in cache[:1])
    return pl.pallas_call(
        kern,
        grid=(b, l // (cps * CHUNK)),
        in_specs=[pl.BlockSpec((1, cps * CHUNK, d), lambda i, c: (i, c, 0)),
                  seq(lk), seq(lk), *[seq(a.shape[1]) for a in cache],
                  _resident(bias.shape)],
        out_specs=pl.BlockSpec((1, cps * CHUNK, d), lambda i, c: (i, c, 0)),
        out_shape=jax.ShapeDtypeStruct((b, l, d), BF16),
        scratch_shapes=[pltpu.VMEM((band_rows, d), BF16), pltpu.VMEM((band_rows, d), BF16)],
        compiler_params=_cparams("parallel", "arbitrary"),
        name="band_attention",
    )(q, k, v, *cache, bias)


def _row(v):
    return v.reshape(1, -1).astype(F32)


def _ssd_layer(x3, state, p, nw, *, q, nb):
    b, l, d = x3.shape
    rows = b * l
    d_inner = p["wout"].shape[0]
    conv_dim = p["convw"].shape[1]
    x = x3.reshape(rows, d)
    z, xbc, dt, dtT = _inproj(x, _row(nw[0]), p["win"], d_inner=d_inner, conv_dim=conv_dim,
                              nh=p["dtb"].shape[1])
    xbc3 = xbc.reshape(b, l, conv_dim)
    if state is not None:
        h0, conv_prev = state
        state = (h0, jnp.pad(conv_prev, ((0, 0), (CARRY_ROWS - (CONV_W - 1), 0), (0, 0))))
    dtT = jnp.transpose(dtT.reshape(-1, rows // q, q), (1, 0, 2))
    y, h_new = _ssd(xbc3, z.reshape(b, l, d_inner), dt, dtT, state,
                 p["convw"], p["convb"], p["dtb"], p["dtbT"], p["alog"], p["alogT"],
                 p["dskip"], p["gnw"], p["expand"], q=q, nb=nb)
    new_conv = xbc3[:, l - (CONV_W - 1):]
    return y.reshape(rows, d_inner), h_new, new_conv


def _layer_tail(mixed, w_mix, x3, fp, nw, name):
    b, l, d = x3.shape
    out = _mix_ffn(mixed, w_mix, _row(nw[1]), x3.reshape(b * l, d), _row(nw[2]),
                   fp["win"], fp["wo"], _row(nw[3]), name)
    return out.reshape(b, l, d)


def kernel(x_prompt, x_sample, state_ssm, state_conv, cache_k, cache_v, norm_w,
           ssm_w_in, ssm_conv_w, ssm_conv_b, ssm_dt_bias, ssm_A_log, ssm_D, ssm_norm_w, ssm_w_out,
           kv_norm_w, w_kv, attn_w_q, attn_rel_bias, attn_w_o, ffn_w_in, ffn_w_out):
    bp, lp, d = x_prompt.shape
    bs, ls, _ = x_sample.shape
    n_a = ssm_w_in.shape[0]
    depth = norm_w.shape[0]
    d_inner = ssm_w_out.shape[1]
    heads = ssm_dt_bias.shape[1]
    conv_dim = ssm_conv_w.shape[2]
    att_dim = attn_w_q.shape[2]
    att_heads = att_dim // ATT_HEAD_DIM
    hidden = ffn_w_out.shape[1]

    xp, xs = x_prompt, x_sample
    ssm_p, conv_p, ssm_s, conv_s = [], [], [], []
    outs_kv = None
    for layer in range(depth):
        nw = norm_w[layer]
        fp = {"win": ffn_w_in[layer].astype(BF16), "wo": ffn_w_out[layer].astype(BF16)}
        if layer < n_a:
            a = layer
            sp = {"win": jnp.pad(ssm_w_in[a], ((0, 0), (0, 2 * SSM_HEAD_DIM - heads))).astype(BF16),
                  "convw": 0.5 * ssm_conv_w[a], "convb": _row(0.5 * ssm_conv_b[a]),
                  "dtb": _row(ssm_dt_bias[a]), "dtbT": ssm_dt_bias[a].reshape(-1, 1),
                  "alog": _row(ssm_A_log[a]), "alogT": ssm_A_log[a].reshape(-1, 1),
                  "dskip": _row(jnp.repeat(ssm_D[a], SSM_HEAD_DIM)),
                  "gnw": _row(ssm_norm_w[a]), "wout": ssm_w_out[a].astype(BF16),
                  "expand": jnp.repeat(jnp.eye(heads, dtype=BF16), SSM_HEAD_DIM, axis=1)}
            mp, hp_new, cp_new = _ssd_layer(xp, None, sp, nw, q=SSD_BLOCK, nb=SSD_BLOCKS_PER_STEP)
            ms, hs_new, cs_new = _ssd_layer(xs, (state_ssm[a], state_conv[a]), sp, nw, q=ls, nb=1)
            w_mix = sp["wout"]
            ssm_p.append(hp_new)
            conv_p.append(cp_new)
            ssm_s.append(hs_new)
            conv_s.append(cs_new)
        else:
            i = layer - n_a
            wq = attn_w_q[i].astype(BF16)
            wkv = w_kv.astype(BF16)
            w_mix = attn_w_o[i].astype(BF16)
            bias = _bias_table(attn_rel_bias[i]).reshape(att_heads // 2, 2 * CHUNK, BAND)
            rows_p = min(BAND_CHUNKS * CHUNK, lp)
            qp, kpb, vpb, kpf, vpf = _qkv(xp, _row(nw[0]), _row(kv_norm_w), wq, wkv,
                                          tm=ROW_TILE, tail_rows=rows_p)
            mp = _attention(qp, kpb, vpb, bias, first_chunk=0,
                            cps=ATTN_CHUNKS_PER_STEP).reshape(bp * lp, att_dim)
            flat = lambda a: a.reshape(bs, ls, att_dim)
            qs, ksb, vsb, ksf, vsf = map(flat, _qkv(xs.reshape(1, bs * ls, d), _row(nw[0]),
                                                    _row(kv_norm_w), wq, wkv, tm=ROW_TILE,
                                                    tail_rows=bs * ls))
            r = cache_k.shape[1]
            cache = (cache_k.astype(BF16).reshape(bs, r, att_dim),
                     cache_v.astype(BF16).reshape(bs, r, att_dim))
            ms = _attention(qs, ksb, vsb, bias, first_chunk=BAND_CHUNKS, cps=1,
                            cache=cache).reshape(bs * ls, att_dim)
            if outs_kv is None:
                outs_kv = (kpf.reshape(bp, rows_p, att_heads, ATT_HEAD_DIM),
                           vpf.reshape(bp, rows_p, att_heads, ATT_HEAD_DIM),
                           ksf.reshape(bs, ls, att_heads, ATT_HEAD_DIM),
                           vsf.reshape(bs, ls, att_heads, ATT_HEAD_DIM))
        xp = _layer_tail(mp, w_mix, xp, fp, nw, "mix_ffn_prompt")
        xs = _layer_tail(ms, w_mix, xs, fp, nw, "mix_ffn_sample")
    kp_out, vp_out, ks_out, vs_out = outs_kv
    return (xp, xs, jnp.stack(ssm_p), jnp.stack(conv_p), kp_out, vp_out,
            jnp.stack(ssm_s), jnp.stack(conv_s), ks_out, vs_out)
```

```python
import functools

import jax
import jax.numpy as jnp
from jax import lax
from jax.experimental import pallas as pl
from jax.experimental.pallas import tpu as pltpu

F32 = jnp.float32
BF16 = jnp.bfloat16

EPS = 1e-6
CHUNK = 64
BAND_CHUNKS = 8
BAND = (BAND_CHUNKS + 1) * CHUNK
REL_CLIP = 256
SSM_HEAD_DIM = 64
SSM_GROUPS = 4
SSM_STATE = 128
CONV_W = 4
ATT_HEAD_DIM = 64
LOG2E = 1.4426950408889634
Q_SCALE = ATT_HEAD_DIM ** -0.5 * LOG2E
CARRY_ROWS = 8

VMEM_LIMIT_BYTES = 56 * 1024 * 1024
ROW_TILE = 512
COL_TILE = 512
ATTN_CHUNKS_PER_STEP = 8
SSD_BLOCK = 128
SSD_BLOCKS_PER_STEP = 4


def _cparams(*sem):
    return pltpu.CompilerParams(dimension_semantics=sem,
                                vmem_limit_bytes=VMEM_LIMIT_BYTES)


def _resident(shape):
    zeros = (0,) * len(shape)
    return pl.BlockSpec(shape, lambda *_: zeros, pipeline_mode=pl.Buffered(1))


def _rms(x, w):
    ms = jnp.mean(x * x, axis=-1, keepdims=True)
    return x * lax.rsqrt(ms + EPS) * w


def _sigmoid(x):
    return 1.0 / (1.0 + jnp.exp2(x * -LOG2E))


def _silu_of_twice(h):
    return h + h * jnp.tanh(h)


def _softplus(x):
    return jnp.maximum(x, 0.0) + jnp.log(1.0 + jnp.exp(-jnp.abs(x)))


def _dot(a, b):
    return jnp.dot(a, b, preferred_element_type=F32)


def _dot_nt(a, b):
    return lax.dot_general(a, b, (((1,), (1,)), ((), ())), preferred_element_type=F32)


def _split3(x):
    hi = x.astype(BF16)
    r1 = x - hi.astype(F32)
    mid = r1.astype(BF16)
    lo = (r1 - mid.astype(F32)).astype(BF16)
    return hi, mid, lo


def _inproj_kernel(x_ref, nw_ref, w_ref, z_ref, xbc_ref, dt_ref, dtT_ref):
    xn = _rms(x_ref[...], nw_ref[...]).astype(BF16)
    d_inner, conv_dim, nh = z_ref.shape[1], xbc_ref.shape[1], dt_ref.shape[1]
    for c0 in range(0, d_inner, COL_TILE):
        z_ref[:, c0:c0 + COL_TILE] = 0.5 * _dot(xn, w_ref[:, c0:c0 + COL_TILE])
    for c0 in range(0, conv_dim, COL_TILE):
        xbc_ref[:, c0:c0 + COL_TILE] = _dot(xn, w_ref[:, d_inner + c0:d_inner + c0 + COL_TILE])
    dt = _dot(xn, w_ref[:, d_inner + conv_dim:])
    dt_ref[...] = dt[:, :nh]
    dtT_ref[...] = dt.T[:nh, :]


def _inproj(x, nw, w, *, d_inner, conv_dim, nh):
    rows, d = x.shape
    tm = ROW_TILE
    return pl.pallas_call(
        _inproj_kernel,
        grid=(rows // tm,),
        in_specs=[pl.BlockSpec((tm, d), lambda i: (i, 0)),
                  _resident(nw.shape), _resident(w.shape)],
        out_specs=[pl.BlockSpec((tm, d_inner), lambda i: (i, 0)),
                   pl.BlockSpec((tm, conv_dim), lambda i: (i, 0)),
                   pl.BlockSpec((tm, nh), lambda i: (i, 0)),
                   pl.BlockSpec((nh, tm), lambda i: (0, i))],
        out_shape=[jax.ShapeDtypeStruct((rows, d_inner), F32),
                   jax.ShapeDtypeStruct((rows, conv_dim), F32),
                   jax.ShapeDtypeStruct((rows, nh), F32),
                   jax.ShapeDtypeStruct((nh, rows), F32)],
        compiler_params=_cparams("parallel"),
        name="ssm_in_proj",
    )(x, nw, w)


def _ssd_block(j, q, xbc_ref, z_ref, dt_ref, dtT_ref,
               convw_ref, convb_ref, dtb_ref, dtbT_ref, alog_ref, alogT_ref, dskip_ref, gnw_ref,
               expand_ref, ybf_ref,
               win_ref, act_ref, xsb_ref, bb_ref, cb_ref, cbm_ref, bT_ref, yacc_ref, state_ref):
    rows = slice(j * q, (j + 1) * q)
    d_inner = ybf_ref.shape[1]
    n = SSM_STATE
    gn = SSM_GROUPS * n
    lanes = 2 * SSM_HEAD_DIM
    pairs = d_inner // lanes
    pairs_per_group = pairs // SSM_GROUPS
    panels = win_ref.shape[0]
    b_off, c_off = d_inner, d_inner + gn

    for pn in range(panels):
        sl = slice(pn * lanes, (pn + 1) * lanes)
        win_ref[pn, CARRY_ROWS:CARRY_ROWS + q, :] = xbc_ref[0, rows, sl]
        conv = convb_ref[:, sl]
        for k in range(CONV_W):
            r0 = CARRY_ROWS - (CONV_W - 1) + k
            conv = conv + win_ref[pn, r0:r0 + q, :] * convw_ref[k:k + 1, sl]
        win_ref[pn, 0:CARRY_ROWS, :] = win_ref[pn, q:q + CARRY_ROWS, :]
        act = _silu_of_twice(conv)
        act_ref[:, sl] = act
        if pn * lanes < b_off:
            xsb_ref[:, sl] = act.astype(BF16)
        elif pn * lanes < c_off:
            bb_ref[:, pn * lanes - b_off:(pn + 1) * lanes - b_off] = act.astype(BF16)
        else:
            cb_ref[:, pn * lanes - c_off:(pn + 1) * lanes - c_off] = act.astype(BF16)

    dt = _softplus(dt_ref[rows, :] + dtb_ref[...])
    dtT = _softplus(dtT_ref[j] + dtbT_ref[...])
    dA = dt * (-LOG2E * jnp.exp(alog_ref[...]))
    dAT = dtT * (-LOG2E * jnp.exp(alogT_ref[...]))
    row = lax.broadcasted_iota(jnp.int32, (q, q), 0)
    col = lax.broadcasted_iota(jnp.int32, (q, q), 1)
    causal = row >= col
    tril = jnp.where(causal, 1.0, 0.0).astype(BF16)
    triu = jnp.where(row <= col, 1.0, 0.0).astype(BF16)
    acs = sum(_dot(tril, piece) for piece in _split3(dA))
    acsT = sum(_dot(piece, triu) for piece in _split3(dAT))
    a_lastT = acsT[:, q - 1:q]
    row_term = acsT - jnp.log(dtT) * LOG2E
    w_rows = dtT * jnp.exp2(a_lastT - acsT)
    d_last = jnp.broadcast_to(jnp.exp2(acs[q - 1:q, :]), (8, acs.shape[1]))
    d_lanes = sum(_dot(piece, expand_ref[...]) for piece in _split3(d_last))[0:1]

    for g in range(SSM_GROUPS):
        gs = slice(g * n, (g + 1) * n)
        cbm_ref[g] = _dot_nt(cb_ref[:, gs], bb_ref[:, gs])
        bT_ref[g] = act_ref[:, b_off + g * n:b_off + (g + 1) * n].T

    left_q = lax.broadcasted_iota(jnp.int32, (q, lanes), 1) < SSM_HEAD_DIM
    left_n = lax.broadcasted_iota(jnp.int32, (n, lanes), 1) < SSM_HEAD_DIM

    def operands(hp):
        g = hp // pairs_per_group
        sl = slice(hp * lanes, (hp + 1) * lanes)
        cbm = cbm_ref[g]
        cg32 = act_ref[:, c_off + g * n:c_off + (g + 1) * n]
        bgT = bT_ref[g]
        tops, btws = [], []
        for h in (2 * hp, 2 * hp + 1):
            a_b = jnp.broadcast_to(acs[:, h:h + 1], (q, n))
            inter = jnp.exp2(a_b) * cg32
            intra = cbm * jnp.exp2(jnp.where(causal, a_b[:, :q] - row_term[h:h + 1, :], -jnp.inf))
            tops.append(jnp.concatenate([inter.astype(BF16), intra.astype(BF16)], axis=1))
            btws.append((bgT * w_rows[h:h + 1, :]).astype(BF16))
        rhs = jnp.concatenate([state_ref[:, sl].astype(BF16), xsb_ref[:, sl]], axis=0)
        return jnp.concatenate(tops, axis=0), jnp.concatenate(btws, axis=0), rhs

    def products(ops):
        lhs_y, lhs_s, rhs = ops
        return _dot(lhs_y, rhs), _dot(lhs_s, rhs[n:])

    def commit(hp, res):
        res_y, res_s = res
        sl = slice(hp * lanes, (hp + 1) * lanes)
        yacc_ref[:, sl] = jnp.where(left_q, res_y[:q], res_y[q:])
        state_ref[:, sl] = state_ref[:, sl] * d_lanes[:, sl] + jnp.where(left_n, res_s[:n], res_s[n:])

    ops = {0: operands(0)}
    for t in range(1, pairs + 1):
        res = products(ops.pop(t - 1))
        if t < pairs:
            ops[t] = operands(t)
        commit(t - 1, res)

    gw = d_inner // SSM_GROUPS
    for g in range(SSM_GROUPS):
        sl = slice(g * gw, (g + 1) * gw)
        zg = z_ref[0, rows, sl]
        yg = (yacc_ref[:, sl] + act_ref[:, sl] * dskip_ref[:, sl]) * _silu_of_twice(zg)
        yg = yg * lax.rsqrt(jnp.mean(yg * yg, axis=-1, keepdims=True) + EPS)
        ybf_ref[rows, sl] = (yg * gnw_ref[:, sl]).astype(ybf_ref.dtype)


def _ssd_kernel(*refs, q, zero_init):
    xbc_ref, z_ref, dt_ref, dtT_ref = refs[:4]
    h0_ref, c0_ref = (None, None) if zero_init else refs[4:6]
    (convw_ref, convb_ref, dtb_ref, dtbT_ref, alog_ref, alogT_ref, dskip_ref, gnw_ref, expand_ref,
     wout_ref, m_ref, hout_ref, win_ref, *scratch, ybf_ref, state_ref) = refs[4 if zero_init else 6:]
    c = pl.program_id(1)
    n = SSM_STATE
    panels, _, lanes = win_ref.shape
    pairs = state_ref.shape[1] // lanes

    @pl.when(c == 0)
    def _():
        if zero_init:
            state_ref[...] = jnp.zeros_like(state_ref)
            win_ref[:, 0:CARRY_ROWS, :] = jnp.zeros((panels, CARRY_ROWS, lanes), F32)
        else:
            for hp in range(pairs):
                state_ref[:, hp * lanes:(hp + 1) * lanes] = (
                    h0_ref[0, 2 * hp:2 * hp + 2].reshape(lanes, n).T)
            for pn in range(panels):
                win_ref[pn, 0:CARRY_ROWS, :] = c0_ref[0, :, pn * lanes:(pn + 1) * lanes]

    for j in range(m_ref.shape[1] // q):
        _ssd_block(j, q, xbc_ref, z_ref, dt_ref, dtT_ref,
                   convw_ref, convb_ref, dtb_ref, dtbT_ref, alog_ref, alogT_ref, dskip_ref,
                   gnw_ref, expand_ref, ybf_ref, win_ref, *scratch, state_ref)
    for c0 in range(0, wout_ref.shape[1], COL_TILE):
        m_ref[0, :, c0:c0 + COL_TILE] = _dot(ybf_ref[...], wout_ref[:, c0:c0 + COL_TILE])

    @pl.when(c == pl.num_programs(1) - 1)
    def _():
        for hp in range(pairs):
            hout_ref[0, 2 * hp:2 * hp + 2] = (
                state_ref[:, hp * lanes:(hp + 1) * lanes].T.reshape(2, SSM_HEAD_DIM, n))


def _ssd(xbc, z, dt, dtT, state, convw, convb, dtb, dtbT, alog, alogT, dskip, gnw, expand, wout,
         *, q, nb):
    b, l, conv_dim = xbc.shape
    d_inner = z.shape[2]
    nh = dt.shape[1]
    rows = nb * q
    nc = l // rows
    gn = SSM_GROUPS * SSM_STATE
    lanes = 2 * SSM_HEAD_DIM
    kern = functools.partial(_ssd_kernel, q=q, zero_init=state is None)
    hshape = (d_inner // SSM_HEAD_DIM, SSM_HEAD_DIM, SSM_STATE)
    hspec = pl.BlockSpec((1, *hshape), lambda i, c: (i, 0, 0, 0))
    state_specs = [] if state is None else [
        hspec, pl.BlockSpec((1, CARRY_ROWS, conv_dim), lambda i, c: (i, 0, 0))]
    return pl.pallas_call(
        kern,
        grid=(b, nc),
        in_specs=[pl.BlockSpec((1, rows, conv_dim), lambda i, c: (i, c, 0)),
                  pl.BlockSpec((1, rows, d_inner), lambda i, c: (i, c, 0)),
                  pl.BlockSpec((rows, nh), lambda i, c: (i * nc + c, 0)),
                  pl.BlockSpec((nb, nh, q), lambda i, c: (i * nc + c, 0, 0)),
                  *state_specs,
                  _resident(convw.shape), _resident(convb.shape),
                  _resident(dtb.shape), _resident(dtbT.shape),
                  _resident(alog.shape), _resident(alogT.shape),
                  _resident(dskip.shape), _resident(gnw.shape), _resident(expand.shape),
                  _resident(wout.shape)],
        out_specs=[pl.BlockSpec((1, rows, wout.shape[1]), lambda i, c: (i, c, 0)), hspec],
        out_shape=[jax.ShapeDtypeStruct((b, l, wout.shape[1]), F32),
                   jax.ShapeDtypeStruct((b, *hshape), F32)],
        scratch_shapes=[pltpu.VMEM((conv_dim // lanes, CARRY_ROWS + q, lanes), F32),
                        pltpu.VMEM((q, conv_dim), F32),
                        pltpu.VMEM((q, d_inner), BF16),
                        pltpu.VMEM((q, gn), BF16),
                        pltpu.VMEM((q, gn), BF16),
                        pltpu.VMEM((SSM_GROUPS, q, q), F32),
                        pltpu.VMEM((SSM_GROUPS, SSM_STATE, q), F32),
                        pltpu.VMEM((q, d_inner), F32),
                        pltpu.VMEM((rows, d_inner), BF16),
                        pltpu.VMEM((SSM_STATE, d_inner), F32)],
        compiler_params=_cparams("parallel", "arbitrary"),
        name="ssd_mixer",
    )(xbc, z, dt, dtT, *(state or ()), convw, convb, dtb, dtbT, alog, alogT, dskip, gnw, expand, wout)


def _mix_ffn_kernel(*refs, hc, parts, projected):
    a_ref = refs[0]
    wmix_ref = None if projected else refs[1]
    (nw_mix_ref, x_ref, nw_in_ref, win_ref, wo_ref,
     nw_out_ref, o_ref, xn_ref, h_ref) = refs[1 if projected else 2:]
    tm = x_ref.shape[0]
    pr = tm // parts
    hidden = wo_ref.shape[0]
    chunks = list(range(0, hidden, hc))

    def mix(p):
        rows = slice(p * pr, (p + 1) * pr)
        mixed = a_ref[rows, :] if projected else _dot(a_ref[rows, :], wmix_ref[...])
        x1 = x_ref[rows, :] + _rms(mixed, nw_mix_ref[...])
        o_ref[rows, :] = x1
        xn_ref[rows, :] = _rms(x1, nw_in_ref[...]).astype(BF16)

    def up(p, cs):
        rows = slice(p * pr, (p + 1) * pr)
        for c0 in cs:
            gate = _dot(xn_ref[rows, :], win_ref[:, c0:c0 + hc])
            upv = _dot(xn_ref[rows, :], win_ref[:, hidden + c0:hidden + c0 + hc])
            h_ref[rows, c0:c0 + hc] = (gate * _sigmoid(gate) * upv).astype(BF16)

    def down(p):
        rows = slice(p * pr, (p + 1) * pr)
        y = _dot(h_ref[rows, :], wo_ref[...])
        o_ref[rows, :] = o_ref[rows, :] + _rms(y, nw_out_ref[...])

    mix(0)
    for p in range(parts):
        up(p, chunks[:1])
        if p + 1 < parts:
            mix(p + 1)
        if p >= 1:
            down(p - 1)
        up(p, chunks[1:])
    down(parts - 1)


def _mix_ffn(a, wmix, nw_mix, x, nw_in, win, wo, nw_out, name):
    rows, d = x.shape
    k = a.shape[1]
    hidden = wo.shape[0]
    tm = ROW_TILE
    kern = functools.partial(_mix_ffn_kernel, hc=256, parts=2, projected=wmix is None)
    mix_w = [] if wmix is None else [wmix]
    return pl.pallas_call(
        kern,
        grid=(rows // tm,),
        in_specs=[pl.BlockSpec((tm, k), lambda i: (i, 0)),
                  *[_resident(w.shape) for w in mix_w], _resident(nw_mix.shape),
                  pl.BlockSpec((tm, d), lambda i: (i, 0)),
                  _resident(nw_in.shape), _resident(win.shape),
                  _resident(wo.shape), _resident(nw_out.shape)],
        out_specs=pl.BlockSpec((tm, d), lambda i: (i, 0)),
        out_shape=jax.ShapeDtypeStruct((rows, d), F32),
        scratch_shapes=[pltpu.VMEM((tm, d), BF16), pltpu.VMEM((tm, hidden), BF16)],
        compiler_params=_cparams("parallel"),
        name=name,
    )(a, *mix_w, nw_mix, x, nw_in, win, wo, nw_out)


def _qkv_kernel(x_ref, nwq_ref, nwkv_ref, wq_ref, wkv_ref,
                q_ref, k_ref, v_ref, kf_ref, vf_ref):
    x = x_ref[0]
    xh = x * lax.rsqrt(jnp.mean(x * x, axis=-1, keepdims=True) + EPS)
    xq = (xh * nwq_ref[...]).astype(BF16)
    xkv = (xh * nwkv_ref[...]).astype(BF16)
    d = wq_ref.shape[1]
    for c0 in range(0, d, COL_TILE):
        sl = slice(c0, c0 + COL_TILE)
        q_ref[0, :, sl] = (_dot(xq, wq_ref[:, sl]) * Q_SCALE).astype(q_ref.dtype)
        kk = _dot(xkv, wkv_ref[:, sl])
        vv = _dot(xkv, wkv_ref[:, d + c0:d + c0 + COL_TILE])
        k_ref[0, :, sl] = kk.astype(k_ref.dtype)
        v_ref[0, :, sl] = vv.astype(v_ref.dtype)
        kf_ref[0, :, sl] = kk
        vf_ref[0, :, sl] = vv


def _qkv(x, nwq, nwkv, wq, wkv, *, tm, tail_rows):
    b, l, d = x.shape
    nblk = l // tm
    tail_blocks = tail_rows // tm
    rows = lambda bi, i: (bi, i, 0)
    tail = lambda bi, i: (bi, jnp.maximum(i - (nblk - tail_blocks), 0), 0)
    return pl.pallas_call(
        _qkv_kernel,
        grid=(b, nblk),
        in_specs=[pl.BlockSpec((1, tm, d), rows),
                  _resident(nwq.shape), _resident(nwkv.shape),
                  _resident(wq.shape), _resident(wkv.shape)],
        out_specs=[pl.BlockSpec((1, tm, d), rows),
                   pl.BlockSpec((1, tm, d), rows),
                   pl.BlockSpec((1, tm, d), rows),
                   pl.BlockSpec((1, tm, d), tail),
                   pl.BlockSpec((1, tm, d), tail)],
        out_shape=[jax.ShapeDtypeStruct((b, l, d), BF16),
                   jax.ShapeDtypeStruct((b, l, d), BF16),
                   jax.ShapeDtypeStruct((b, l, d), BF16),
                   jax.ShapeDtypeStruct((b, tail_rows, d), F32),
                   jax.ShapeDtypeStruct((b, tail_rows, d), F32)],
        compiler_params=_cparams("parallel", "arbitrary"),
        name="attn_qkv_proj",
    )(x, nwq, nwkv, wq, wkv)


def _bias_kernel(u_ref, o_ref):
    heads, width = u_ref.shape
    for h in range(heads):
        x = jnp.broadcast_to(u_ref[h:h + 1, :] * LOG2E, (CHUNK, width))
        r = pltpu.roll(x, width - CHUNK, 1, stride=1, stride_axis=0)
        o_ref[h] = r[:, :BAND]


def _bias_table(rel_bias):
    heads = rel_bias.shape[0]
    width = BAND + CHUNK
    far = jnp.broadcast_to(rel_bias[:, 2 * REL_CLIP:], (heads, width - (REL_CLIP + CHUNK)))
    near = jnp.flip(rel_bias[:, REL_CLIP - CHUNK + 1:], axis=1)
    u = jnp.concatenate([far, near], axis=1)
    return pl.pallas_call(
        _bias_kernel,
        out_shape=jax.ShapeDtypeStruct((heads, CHUNK, BAND), F32),
        name="attn_rel_bias_table",
    )(u)


def _attn_kernel(*refs, first_chunk, cps, with_cache):
    q_ref, k_ref, v_ref = refs[:3]
    ck_ref, cv_ref = refs[3:5] if with_cache else (None, None)
    bias_ref, o_ref, kbuf_ref, vbuf_ref = refs[5 if with_cache else 3:]
    step = pl.program_id(1)

    @pl.when(step == 0)
    def _():
        r = CHUNK + (ck_ref.shape[1] if with_cache else 0)
        kbuf_ref[0:CHUNK, :] = jnp.zeros((CHUNK, kbuf_ref.shape[1]), BF16)
        vbuf_ref[0:CHUNK, :] = jnp.zeros((CHUNK, vbuf_ref.shape[1]), BF16)
        if with_cache:
            kbuf_ref[CHUNK:r, :] = ck_ref[0].astype(BF16)
            vbuf_ref[CHUNK:r, :] = cv_ref[0].astype(BF16)
        kbuf_ref[r:, :] = k_ref[0]
        vbuf_ref[r:, :] = v_ref[0]

    pairs = bias_ref.shape[0]
    pw = 2 * ATT_HEAD_DIM
    left = lax.broadcasted_iota(jnp.int32, (CHUNK, pw), 1) < ATT_HEAD_DIM
    units = [(ci, hp) for ci in range(cps) for hp in range(pairs)]

    def body(static_step):
        def geometry(u):
            if static_step is None:
                return BAND_CHUNKS + 1, False
            valid = min(static_step * cps + units[u][0] + first_chunk + 1, BAND_CHUNKS + 1)
            return valid + (1 - valid % 2), valid % 2 == 0

        def band(ref, u):
            ci, hp = units[u]
            take = geometry(u)[0] * CHUNK
            if static_step is None:
                first = (step * cps + ci + first_chunk - BAND_CHUNKS + 1) * CHUNK
                rows = pl.ds(pl.multiple_of(first, CHUNK), take)
            else:
                end = (static_step * cps + ci + first_chunk + 2) * CHUNK
                rows = slice(end - take, end)
            return ref[rows, hp * pw:(hp + 1) * pw]

        def scores(u):
            ci, hp = units[u]
            chunks, masked = geometry(u)
            qp = q_ref[0, ci * CHUNK:(ci + 1) * CHUNK, hp * pw:(hp + 1) * pw]
            zero = jnp.zeros_like(qp)
            q2 = jnp.concatenate([jnp.where(left, qp, zero), jnp.where(left, zero, qp)], axis=0)
            s = _dot_nt(q2, band(kbuf_ref, u)) + bias_ref[hp, :, BAND - chunks * CHUNK:]
            if masked:
                lane = lax.broadcasted_iota(jnp.int32, s.shape, 1)
                s = jnp.where(lane >= CHUNK, s, -jnp.inf)
            return s

        def softmax(s):
            pr = jnp.exp2(s - jnp.max(s, axis=-1, keepdims=True))
            return pr.astype(BF16), jnp.sum(pr, axis=-1, keepdims=True)

        def weighted(u, pr):
            v = band(vbuf_ref, u)
            past = (geometry(u)[0] - 1) * CHUNK
            own = _dot(pr[:, past:], v[past:])
            return own if past == 0 else _dot(pr[:, :past], v[:past]) + own

        def finish(u, r, denom):
            ci, hp = units[u]
            r = r / denom
            o_ref[0, ci * CHUNK:(ci + 1) * CHUNK, hp * pw:(hp + 1) * pw] = (
                jnp.where(left, r[:CHUNK], r[CHUNK:]).astype(o_ref.dtype))

        n = len(units)
        s, pr, den, acc = {}, {}, {}, {}
        for t in range(n + 3):
            if t < n:
                s[t] = scores(t)
            if 0 <= t - 2 < n:
                acc[t - 2] = weighted(t - 2, pr.pop(t - 2))
            if 0 <= t - 1 < n:
                pr[t - 1], den[t - 1] = softmax(s.pop(t - 1))
            if 0 <= t - 3 < n:
                finish(t - 3, acc.pop(t - 3), den.pop(t - 3))

    short_steps = max(0, BAND_CHUNKS - first_chunk) // cps
    for st in range(short_steps):
        pl.when(step == st)(functools.partial(body, st))
    pl.when(step >= short_steps)(functools.partial(body, None))


def _attention(q, k, v, bias, *, first_chunk, cps, cache=None):
    b, l, d = q.shape
    lk = k.shape[1]
    assert BAND_CHUNKS % cps == 0 and l % (cps * CHUNK) == 0
    kern = functools.partial(_attn_kernel, first_chunk=first_chunk, cps=cps,
                             with_cache=cache is not None)
    cache = list(cache or ())
    seq = lambda rows: pl.BlockSpec((1, rows, d), lambda i, c: (i, 0, 0))
    band_rows = CHUNK + lk + sum(a.shape[1] for a in cache[:1])
    return pl.pallas_call(
        kern,
        grid=(b, l // (cps * CHUNK)),
        in_specs=[pl.BlockSpec((1, cps * CHUNK, d), lambda i, c: (i, c, 0)),
                  seq(lk), seq(lk), *[seq(a.shape[1]) for a in cache],
                  _resident(bias.shape)],
        out_specs=pl.BlockSpec((1, cps * CHUNK, d), lambda i, c: (i, c, 0)),
        out_shape=jax.ShapeDtypeStruct((b, l, d), BF16),
        scratch_shapes=[pltpu.VMEM((band_rows, d), BF16), pltpu.VMEM((band_rows, d), BF16)],
        compiler_params=_cparams("parallel", "arbitrary"),
        name="band_attention",
    )(q, k, v, *cache, bias)


def _row(v):
    return v.reshape(1, -1).astype(F32)


def _ssd_layer(x3, state, p, nw, *, q, nb):
    b, l, d = x3.shape
    rows = b * l
    d_inner = p["wout"].shape[0]
    conv_dim = p["convw"].shape[1]
    x = x3.reshape(rows, d)
    z, xbc, dt, dtT = _inproj(x, _row(nw[0]), p["win"], d_inner=d_inner, conv_dim=conv_dim,
                              nh=p["dtb"].shape[1])
    xbc3 = xbc.reshape(b, l, conv_dim)
    if state is not None:
        h0, conv_prev = state
        state = (h0, jnp.pad(conv_prev, ((0, 0), (CARRY_ROWS - (CONV_W - 1), 0), (0, 0))))
    dtT = jnp.transpose(dtT.reshape(-1, rows // q, q), (1, 0, 2))
    y, h_new = _ssd(xbc3, z.reshape(b, l, d_inner), dt, dtT, state,
                 p["convw"], p["convb"], p["dtb"], p["dtbT"], p["alog"], p["alogT"],
                 p["dskip"], p["gnw"], p["expand"], p["wout"], q=q, nb=nb)
    new_conv = xbc3[:, l - (CONV_W - 1):]
    return y.reshape(rows, d), h_new, new_conv


def _layer_tail(mixed, w_mix, x3, fp, nw, name):
    b, l, d = x3.shape
    out = _mix_ffn(mixed, w_mix, _row(nw[1]), x3.reshape(b * l, d), _row(nw[2]),
                   fp["win"], fp["wo"], _row(nw[3]), name)
    return out.reshape(b, l, d)


def kernel(x_prompt, x_sample, state_ssm, state_conv, cache_k, cache_v, norm_w,
           ssm_w_in, ssm_conv_w, ssm_conv_b, ssm_dt_bias, ssm_A_log, ssm_D, ssm_norm_w, ssm_w_out,
           kv_norm_w, w_kv, attn_w_q, attn_rel_bias, attn_w_o, ffn_w_in, ffn_w_out):
    bp, lp, d = x_prompt.shape
    bs, ls, _ = x_sample.shape
    n_a = ssm_w_in.shape[0]
    depth = norm_w.shape[0]
    d_inner = ssm_w_out.shape[1]
    heads = ssm_dt_bias.shape[1]
    conv_dim = ssm_conv_w.shape[2]
    att_dim = attn_w_q.shape[2]
    att_heads = att_dim // ATT_HEAD_DIM
    hidden = ffn_w_out.shape[1]

    xp, xs = x_prompt, x_sample
    ssm_p, conv_p, ssm_s, conv_s = [], [], [], []
    outs_kv = None
    for layer in range(depth):
        nw = norm_w[layer]
        fp = {"win": ffn_w_in[layer].astype(BF16), "wo": ffn_w_out[layer].astype(BF16)}
        if layer < n_a:
            a = layer
            sp = {"win": jnp.pad(ssm_w_in[a], ((0, 0), (0, 2 * SSM_HEAD_DIM - heads))).astype(BF16),
                  "convw": 0.5 * ssm_conv_w[a], "convb": _row(0.5 * ssm_conv_b[a]),
                  "dtb": _row(ssm_dt_bias[a]), "dtbT": ssm_dt_bias[a].reshape(-1, 1),
                  "alog": _row(ssm_A_log[a]), "alogT": ssm_A_log[a].reshape(-1, 1),
                  "dskip": _row(jnp.repeat(ssm_D[a], SSM_HEAD_DIM)),
                  "gnw": _row(ssm_norm_w[a]), "wout": ssm_w_out[a].astype(BF16),
                  "expand": jnp.repeat(jnp.eye(heads, dtype=BF16), SSM_HEAD_DIM, axis=1)}
            mp, hp_new, cp_new = _ssd_layer(xp, None, sp, nw, q=SSD_BLOCK, nb=SSD_BLOCKS_PER_STEP)
            ms, hs_new, cs_new = _ssd_layer(xs, (state_ssm[a], state_conv[a]), sp, nw, q=ls, nb=1)
            w_mix = None
            ssm_p.append(hp_new)
            conv_p.append(cp_new)
            ssm_s.append(hs_new)
            conv_s.append(cs_new)
        else:
            i = layer - n_a
            wq = attn_w_q[i].astype(BF16)
            wkv = w_kv.astype(BF16)
            w_mix = attn_w_o[i].astype(BF16)
            bias = _bias_table(attn_rel_bias[i]).reshape(att_heads // 2, 2 * CHUNK, BAND)
            rows_p = min(BAND_CHUNKS * CHUNK, lp)
            qp, kpb, vpb, kpf, vpf = _qkv(xp, _row(nw[0]), _row(kv_norm_w), wq, wkv,
                                          tm=ROW_TILE, tail_rows=rows_p)
            mp = _attention(qp, kpb, vpb, bias, first_chunk=0,
                            cps=ATTN_CHUNKS_PER_STEP).reshape(bp * lp, att_dim)
            flat = lambda a: a.reshape(bs, ls, att_dim)
            qs, ksb, vsb, ksf, vsf = map(flat, _qkv(xs.reshape(1, bs * ls, d), _row(nw[0]),
                                                    _row(kv_norm_w), wq, wkv, tm=ROW_TILE,
                                                    tail_rows=bs * ls))
            r = cache_k.shape[1]
            cache = (cache_k.astype(BF16).reshape(bs, r, att_dim),
                     cache_v.astype(BF16).reshape(bs, r, att_dim))
            ms = _attention(qs, ksb, vsb, bias, first_chunk=BAND_CHUNKS, cps=1,
                            cache=cache).reshape(bs * ls, att_dim)
            if outs_kv is None:
                outs_kv = (kpf.reshape(bp, rows_p, att_heads, ATT_HEAD_DIM),
                           vpf.reshape(bp, rows_p, att_heads, ATT_HEAD_DIM),
                           ksf.reshape(bs, ls, att_heads, ATT_HEAD_DIM),
                           vsf.reshape(bs, ls, att_heads, ATT_HEAD_DIM))
        xp = _layer_tail(mp, w_mix, xp, fp, nw, "mix_ffn_prompt")
        xs = _layer_tail(ms, w_mix, xs, fp, nw, "mix_ffn_sample")
    kp_out, vp_out, ks_out, vs_out = outs_kv
    return (xp, xs, jnp.stack(ssm_p), jnp.stack(conv_p), kp_out, vp_out,
            jnp.stack(ssm_s), jnp.stack(conv_s), ks_out, vs_out)
```

```python
import functools

import jax
import jax.numpy as jnp
from jax import lax
from jax.experimental import pallas as pl
from jax.experimental.pallas import tpu as pltpu

F32 = jnp.float32
BF16 = jnp.bfloat16

EPS = 1e-6
CHUNK = 64
BAND_CHUNKS = 8
BAND = (BAND_CHUNKS + 1) * CHUNK
REL_CLIP = 256
SSM_HEAD_DIM = 64
SSM_GROUPS = 4
SSM_STATE = 128
CONV_W = 4
ATT_HEAD_DIM = 64
LOG2E = 1.4426950408889634
Q_SCALE = ATT_HEAD_DIM ** -0.5 * LOG2E
CARRY_ROWS = 8

VMEM_LIMIT_BYTES = 56 * 1024 * 1024
ROW_TILE = 512
COL_TILE = 512
ATTN_CHUNKS_PER_STEP = 8
SSD_BLOCK = 128
SSD_BLOCKS_PER_STEP = 4


def _cparams(*sem):
    return pltpu.CompilerParams(dimension_semantics=sem,
                                vmem_limit_bytes=VMEM_LIMIT_BYTES)


def _resident(shape):
    zeros = (0,) * len(shape)
    return pl.BlockSpec(shape, lambda *_: zeros, pipeline_mode=pl.Buffered(1))


def _rms(x, w):
    ms = jnp.mean(x * x, axis=-1, keepdims=True)
    return x * lax.rsqrt(ms + EPS) * w


def _sigmoid(x):
    return 1.0 / (1.0 + jnp.exp2(x * -LOG2E))


def _silu_of_twice(h):
    return h + h * jnp.tanh(h)


def _softplus(x):
    return jnp.maximum(x, 0.0) + jnp.log(1.0 + jnp.exp(-jnp.abs(x)))


def _dot(a, b):
    return jnp.dot(a, b, preferred_element_type=F32)


def _dot_nt(a, b):
    return lax.dot_general(a, b, (((1,), (1,)), ((), ())), preferred_element_type=F32)


def _split3(x):
    hi = x.astype(BF16)
    r1 = x - hi.astype(F32)
    mid = r1.astype(BF16)
    lo = (r1 - mid.astype(F32)).astype(BF16)
    return hi, mid, lo


def _inproj_kernel(x_ref, nw_ref, w_ref, z_ref, xbc_ref, dt_ref, dtT_ref):
    xn = _rms(x_ref[...], nw_ref[...]).astype(BF16)
    d_inner, conv_dim, nh = z_ref.shape[1], xbc_ref.shape[1], dt_ref.shape[1]
    for c0 in range(0, d_inner, COL_TILE):
        z_ref[:, c0:c0 + COL_TILE] = 0.5 * _dot(xn, w_ref[:, c0:c0 + COL_TILE])
    for c0 in range(0, conv_dim, COL_TILE):
        xbc_ref[:, c0:c0 + COL_TILE] = _dot(xn, w_ref[:, d_inner + c0:d_inner + c0 + COL_TILE])
    dt = _dot(xn, w_ref[:, d_inner + conv_dim:])
    dt_ref[...] = dt[:, :nh]
    dtT_ref[...] = dt.T[:nh, :]


def _inproj(x, nw, w, *, d_inner, conv_dim, nh):
    rows, d = x.shape
    tm = ROW_TILE
    return pl.pallas_call(
        _inproj_kernel,
        grid=(rows // tm,),
        in_specs=[pl.BlockSpec((tm, d), lambda i: (i, 0)),
                  _resident(nw.shape), _resident(w.shape)],
        out_specs=[pl.BlockSpec((tm, d_inner), lambda i: (i, 0)),
                   pl.BlockSpec((tm, conv_dim), lambda i: (i, 0)),
                   pl.BlockSpec((tm, nh), lambda i: (i, 0)),
                   pl.BlockSpec((nh, tm), lambda i: (0, i))],
        out_shape=[jax.ShapeDtypeStruct((rows, d_inner), F32),
                   jax.ShapeDtypeStruct((rows, conv_dim), F32),
                   jax.ShapeDtypeStruct((rows, nh), F32),
                   jax.ShapeDtypeStruct((nh, rows), F32)],
        compiler_params=_cparams("parallel"),
        name="ssm_in_proj",
    )(x, nw, w)


def _ssd_block(j, q, xbc_ref, z_ref, dt_ref, dtT_ref,
               convw_ref, convb_ref, dtb_ref, dtbT_ref, alog_ref, alogT_ref, dskip_ref, gnw_ref,
               expand_ref, ybf_ref,
               win_ref, act_ref, xsb_ref, bb_ref, cb_ref, cbm_ref, bT_ref, yacc_ref, state_ref):
    rows = slice(j * q, (j + 1) * q)
    d_inner = ybf_ref.shape[1]
    n = SSM_STATE
    gn = SSM_GROUPS * n
    lanes = 2 * SSM_HEAD_DIM
    pairs = d_inner // lanes
    pairs_per_group = pairs // SSM_GROUPS
    panels = win_ref.shape[0]
    b_off, c_off = d_inner, d_inner + gn

    for pn in range(panels):
        sl = slice(pn * lanes, (pn + 1) * lanes)
        win_ref[pn, CARRY_ROWS:CARRY_ROWS + q, :] = xbc_ref[0, rows, sl]
        conv = convb_ref[:, sl]
        for k in range(CONV_W):
            r0 = CARRY_ROWS - (CONV_W - 1) + k
            conv = conv + win_ref[pn, r0:r0 + q, :] * convw_ref[k:k + 1, sl]
        win_ref[pn, 0:CARRY_ROWS, :] = win_ref[pn, q:q + CARRY_ROWS, :]
        act = _silu_of_twice(conv)
        act_ref[:, sl] = act
        if pn * lanes < b_off:
            xsb_ref[:, sl] = act.astype(BF16)
        elif pn * lanes < c_off:
            bb_ref[:, pn * lanes - b_off:(pn + 1) * lanes - b_off] = act.astype(BF16)
        else:
            cb_ref[:, pn * lanes - c_off:(pn + 1) * lanes - c_off] = act.astype(BF16)

    dt = _softplus(dt_ref[rows, :] + dtb_ref[...])
    dtT = _softplus(dtT_ref[j] + dtbT_ref[...])
    dA = dt * (-LOG2E * jnp.exp(alog_ref[...]))
    dAT = dtT * (-LOG2E * jnp.exp(alogT_ref[...]))
    row = lax.broadcasted_iota(jnp.int32, (q, q), 0)
    col = lax.broadcasted_iota(jnp.int32, (q, q), 1)
    causal = row >= col
    tril = jnp.where(causal, 1.0, 0.0).astype(BF16)
    triu = jnp.where(row <= col, 1.0, 0.0).astype(BF16)
    acs = sum(_dot(tril, piece) for piece in _split3(dA))
    acsT = sum(_dot(piece, triu) for piece in _split3(dAT))
    a_lastT = acsT[:, q - 1:q]
    row_term = acsT - jnp.log(dtT) * LOG2E
    w_rows = dtT * jnp.exp2(a_lastT - acsT)
    d_last = jnp.broadcast_to(jnp.exp2(acs[q - 1:q, :]), (8, acs.shape[1]))
    d_lanes = sum(_dot(piece, expand_ref[...]) for piece in _split3(d_last))[0:1]

    for g in range(SSM_GROUPS):
        gs = slice(g * n, (g + 1) * n)
        cbm_ref[g] = _dot_nt(cb_ref[:, gs], bb_ref[:, gs])
        bT_ref[g] = act_ref[:, b_off + g * n:b_off + (g + 1) * n].T

    left_q = lax.broadcasted_iota(jnp.int32, (q, lanes), 1) < SSM_HEAD_DIM
    left_n = lax.broadcasted_iota(jnp.int32, (n, lanes), 1) < SSM_HEAD_DIM

    def operands(hp):
        g = hp // pairs_per_group
        sl = slice(hp * lanes, (hp + 1) * lanes)
        cbm = cbm_ref[g]
        cg32 = act_ref[:, c_off + g * n:c_off + (g + 1) * n]
        bgT = bT_ref[g]
        tops, btws = [], []
        for h in (2 * hp, 2 * hp + 1):
            a_b = jnp.broadcast_to(acs[:, h:h + 1], (q, n))
            inter = jnp.exp2(a_b) * cg32
            intra = cbm * jnp.exp2(jnp.where(causal, a_b[:, :q] - row_term[h:h + 1, :], -jnp.inf))
            tops.append(jnp.concatenate([inter.astype(BF16), intra.astype(BF16)], axis=1))
            btws.append((bgT * w_rows[h:h + 1, :]).astype(BF16))
        rhs = jnp.concatenate([state_ref[:, sl].astype(BF16), xsb_ref[:, sl]], axis=0)
        return jnp.concatenate(tops, axis=0), jnp.concatenate(btws, axis=0), rhs

    def products(ops):
        lhs_y, lhs_s, rhs = ops
        return _dot(lhs_y, rhs), _dot(lhs_s, rhs[n:])

    def commit(hp, res):
        res_y, res_s = res
        sl = slice(hp * lanes, (hp + 1) * lanes)
        yacc_ref[:, sl] = jnp.where(left_q, res_y[:q], res_y[q:])
        state_ref[:, sl] = state_ref[:, sl] * d_lanes[:, sl] + jnp.where(left_n, res_s[:n], res_s[n:])

    ops = {0: operands(0)}
    for t in range(1, pairs + 1):
        res = products(ops.pop(t - 1))
        if t < pairs:
            ops[t] = operands(t)
        commit(t - 1, res)

    gw = d_inner // SSM_GROUPS
    for g in range(SSM_GROUPS):
        sl = slice(g * gw, (g + 1) * gw)
        zg = z_ref[0, rows, sl]
        yg = (yacc_ref[:, sl] + act_ref[:, sl] * dskip_ref[:, sl]) * _silu_of_twice(zg)
        yg = yg * lax.rsqrt(jnp.mean(yg * yg, axis=-1, keepdims=True) + EPS)
        ybf_ref[rows, sl] = (yg * gnw_ref[:, sl]).astype(ybf_ref.dtype)


def _ssd_kernel(*refs, q, zero_init):
    xbc_ref, z_ref, dt_ref, dtT_ref = refs[:4]
    h0_ref, c0_ref = (None, None) if zero_init else refs[4:6]
    (convw_ref, convb_ref, dtb_ref, dtbT_ref, alog_ref, alogT_ref, dskip_ref, gnw_ref, expand_ref,
     wout_ref, m_ref, hout_ref, win_ref, *scratch, ybf_ref, state_ref) = refs[4 if zero_init else 6:]
    c = pl.program_id(1)
    n = SSM_STATE
    panels, _, lanes = win_ref.shape
    pairs = state_ref.shape[1] // lanes

    @pl.when(c == 0)
    def _():
        if zero_init:
            state_ref[...] = jnp.zeros_like(state_ref)
            win_ref[:, 0:CARRY_ROWS, :] = jnp.zeros((panels, CARRY_ROWS, lanes), F32)
        else:
            for hp in range(pairs):
                state_ref[:, hp * lanes:(hp + 1) * lanes] = (
                    h0_ref[0, 2 * hp:2 * hp + 2].reshape(lanes, n).T)
            for pn in range(panels):
                win_ref[pn, 0:CARRY_ROWS, :] = c0_ref[0, :, pn * lanes:(pn + 1) * lanes]

    for j in range(m_ref.shape[1] // q):
        _ssd_block(j, q, xbc_ref, z_ref, dt_ref, dtT_ref,
                   convw_ref, convb_ref, dtb_ref, dtbT_ref, alog_ref, alogT_ref, dskip_ref,
                   gnw_ref, expand_ref, ybf_ref, win_ref, *scratch, state_ref)
    for c0 in range(0, wout_ref.shape[1], COL_TILE):
        m_ref[0, :, c0:c0 + COL_TILE] = _dot(ybf_ref[...], wout_ref[:, c0:c0 + COL_TILE])

    @pl.when(c == pl.num_programs(1) - 1)
    def _():
        for hp in range(pairs):
            hout_ref[0, 2 * hp:2 * hp + 2] = (
                state_ref[:, hp * lanes:(hp + 1) * lanes].T.reshape(2, SSM_HEAD_DIM, n))


def _ssd(xbc, z, dt, dtT, state, convw, convb, dtb, dtbT, alog, alogT, dskip, gnw, expand, wout,
         *, q, nb):
    b, l, conv_dim = xbc.shape
    d_inner = z.shape[2]
    nh = dt.shape[1]
    rows = nb * q
    nc = l // rows
    gn = SSM_GROUPS * SSM_STATE
    lanes = 2 * SSM_HEAD_DIM
    kern = functools.partial(_ssd_kernel, q=q, zero_init=state is None)
    hshape = (d_inner // SSM_HEAD_DIM, SSM_HEAD_DIM, SSM_STATE)
    hspec = pl.BlockSpec((1, *hshape), lambda i, c: (i, 0, 0, 0))
    state_specs = [] if state is None else [
        hspec, pl.BlockSpec((1, CARRY_ROWS, conv_dim), lambda i, c: (i, 0, 0))]
    return pl.pallas_call(
        kern,
        grid=(b, nc),
        in_specs=[pl.BlockSpec((1, rows, conv_dim), lambda i, c: (i, c, 0)),
                  pl.BlockSpec((1, rows, d_inner), lambda i, c: (i, c, 0)),
                  pl.BlockSpec((rows, nh), lambda i, c: (i * nc + c, 0)),
                  pl.BlockSpec((nb, nh, q), lambda i, c: (i * nc + c, 0, 0)),
                  *state_specs,
                  _resident(convw.shape), _resident(convb.shape),
                  _resident(dtb.shape), _resident(dtbT.shape),
                  _resident(alog.shape), _resident(alogT.shape),
                  _resident(dskip.shape), _resident(gnw.shape), _resident(expand.shape),
                  _resident(wout.shape)],
        out_specs=[pl.BlockSpec((1, rows, wout.shape[1]), lambda i, c: (i, c, 0)), hspec],
        out_shape=[jax.ShapeDtypeStruct((b, l, wout.shape[1]), F32),
                   jax.ShapeDtypeStruct((b, *hshape), F32)],
        scratch_shapes=[pltpu.VMEM((conv_dim // lanes, CARRY_ROWS + q, lanes), F32),
                        pltpu.VMEM((q, conv_dim), F32),
                        pltpu.VMEM((q, d_inner), BF16),
                        pltpu.VMEM((q, gn), BF16),
                        pltpu.VMEM((q, gn), BF16),
                        pltpu.VMEM((SSM_GROUPS, q, q), F32),
                        pltpu.VMEM((SSM_GROUPS, SSM_STATE, q), F32),
                        pltpu.VMEM((q, d_inner), F32),
                        pltpu.VMEM((rows, d_inner), BF16),
                        pltpu.VMEM((SSM_STATE, d_inner), F32)],
        compiler_params=_cparams("parallel", "arbitrary"),
        name="ssd_mixer",
    )(xbc, z, dt, dtT, *(state or ()), convw, convb, dtb, dtbT, alog, alogT, dskip, gnw, expand, wout)


def _mix_ffn_kernel(*refs, hc, parts, projected):
    a_ref = refs[0]
    wmix_ref = None if projected else refs[1]
    (nw_mix_ref, x_ref, nw_in_ref, win_ref, wo_ref,
     nw_out_ref, o_ref, xn_ref, h_ref) = refs[1 if projected else 2:]
    tm = x_ref.shape[0]
    pr = tm // parts
    hidden = wo_ref.shape[0]
    chunks = list(range(0, hidden, hc))

    def mix(p):
        rows = slice(p * pr, (p + 1) * pr)
        mixed = a_ref[rows, :] if projected else _dot(a_ref[rows, :], wmix_ref[...])
        x1 = x_ref[rows, :] + _rms(mixed, nw_mix_ref[...])
        o_ref[rows, :] = x1
        xn_ref[rows, :] = _rms(x1, nw_in_ref[...]).astype(BF16)

    def up(p, cs):
        rows = slice(p * pr, (p + 1) * pr)
        for c0 in cs:
            gate = _dot(xn_ref[rows, :], win_ref[:, c0:c0 + hc])
            upv = _dot(xn_ref[rows, :], win_ref[:, hidden + c0:hidden + c0 + hc])
            h_ref[rows, c0:c0 + hc] = (gate * _sigmoid(gate) * upv).astype(BF16)

    def down(p):
        rows = slice(p * pr, (p + 1) * pr)
        y = _dot(h_ref[rows, :], wo_ref[...])
        o_ref[rows, :] = o_ref[rows, :] + _rms(y, nw_out_ref[...])

    mix(0)
    for p in range(parts):
        up(p, chunks[:1])
        if p + 1 < parts:
            mix(p + 1)
        if p >= 1:
            down(p - 1)
        up(p, chunks[1:])
    down(parts - 1)


def _mix_ffn(a, wmix, nw_mix, x, nw_in, win, wo, nw_out, name):
    rows, d = x.shape
    k = a.shape[1]
    hidden = wo.shape[0]
    tm = ROW_TILE
    kern = functools.partial(_mix_ffn_kernel, hc=256, parts=2, projected=wmix is None)
    mix_w = [] if wmix is None else [wmix]
    return pl.pallas_call(
        kern,
        grid=(rows // tm,),
        in_specs=[pl.BlockSpec((tm, k), lambda i: (i, 0)),
                  *[_resident(w.shape) for w in mix_w], _resident(nw_mix.shape),
                  pl.BlockSpec((tm, d), lambda i: (i, 0)),
                  _resident(nw_in.shape), _resident(win.shape),
                  _resident(wo.shape), _resident(nw_out.shape)],
        out_specs=pl.BlockSpec((tm, d), lambda i: (i, 0)),
        out_shape=jax.ShapeDtypeStruct((rows, d), F32),
        scratch_shapes=[pltpu.VMEM((tm, d), BF16), pltpu.VMEM((tm, hidden), BF16)],
        compiler_params=_cparams("parallel"),
        name=name,
    )(a, *mix_w, nw_mix, x, nw_in, win, wo, nw_out)


def _qkv_kernel(x_ref, nwq_ref, nwkv_ref, wq_ref, wkv_ref,
                q_ref, k_ref, v_ref, kf_ref, vf_ref):
    x = x_ref[0]
    xh = x * lax.rsqrt(jnp.mean(x * x, axis=-1, keepdims=True) + EPS)
    xq = (xh * nwq_ref[...]).astype(BF16)
    xkv = (xh * nwkv_ref[...]).astype(BF16)
    d = wq_ref.shape[1]
    for c0 in range(0, d, COL_TILE):
        sl = slice(c0, c0 + COL_TILE)
        q_ref[0, :, sl] = (_dot(xq, wq_ref[:, sl]) * Q_SCALE).astype(q_ref.dtype)
        kk = _dot(xkv, wkv_ref[:, sl])
        vv = _dot(xkv, wkv_ref[:, d + c0:d + c0 + COL_TILE])
        k_ref[0, :, sl] = kk.astype(k_ref.dtype)
        v_ref[0, :, sl] = vv.astype(v_ref.dtype)
        kf_ref[0, :, sl] = kk
        vf_ref[0, :, sl] = vv


def _qkv(x, nwq, nwkv, wq, wkv, *, tm, tail_rows):
    b, l, d = x.shape
    nblk = l // tm
    tail_blocks = tail_rows // tm
    rows = lambda bi, i: (bi, i, 0)
    tail = lambda bi, i: (bi, jnp.maximum(i - (nblk - tail_blocks), 0), 0)
    return pl.pallas_call(
        _qkv_kernel,
        grid=(b, nblk),
        in_specs=[pl.BlockSpec((1, tm, d), rows),
                  _resident(nwq.shape), _resident(nwkv.shape),
                  _resident(wq.shape), _resident(wkv.shape)],
        out_specs=[pl.BlockSpec((1, tm, d), rows),
                   pl.BlockSpec((1, tm, d), rows),
                   pl.BlockSpec((1, tm, d), rows),
                   pl.BlockSpec((1, tm, d), tail),
                   pl.BlockSpec((1, tm, d), tail)],
        out_shape=[jax.ShapeDtypeStruct((b, l, d), BF16),
                   jax.ShapeDtypeStruct((b, l, d), BF16),
                   jax.ShapeDtypeStruct((b, l, d), BF16),
                   jax.ShapeDtypeStruct((b, tail_rows, d), F32),
                   jax.ShapeDtypeStruct((b, tail_rows, d), F32)],
        compiler_params=_cparams("parallel", "arbitrary"),
        name="attn_qkv_proj",
    )(x, nwq, nwkv, wq, wkv)


def _bias_kernel(u_ref, o_ref):
    heads, width = u_ref.shape
    for h in range(heads):
        x = jnp.broadcast_to(u_ref[h:h + 1, :] * LOG2E, (CHUNK, width))
        r = pltpu.roll(x, width - CHUNK, 1, stride=1, stride_axis=0)
        o_ref[h] = r[:, :BAND]


def _bias_table(rel_bias):
    heads = rel_bias.shape[0]
    width = BAND + CHUNK
    far = jnp.broadcast_to(rel_bias[:, 2 * REL_CLIP:], (heads, width - (REL_CLIP + CHUNK)))
    near = jnp.flip(rel_bias[:, REL_CLIP - CHUNK + 1:], axis=1)
    u = jnp.concatenate([far, near], axis=1)
    return pl.pallas_call(
        _bias_kernel,
        out_shape=jax.ShapeDtypeStruct((heads, CHUNK, BAND), F32),
        name="attn_rel_bias_table",
    )(u)


def _attn_kernel(*refs, first_chunk, cps, with_cache):
    q_ref, k_ref, v_ref = refs[:3]
    ck_ref, cv_ref = refs[3:5] if with_cache else (None, None)
    bias_ref, wo_ref, m_ref, kbuf_ref, vbuf_ref, obuf_ref = refs[5 if with_cache else 3:]
    step = pl.program_id(1)

    @pl.when(step == 0)
    def _():
        r = CHUNK + (ck_ref.shape[1] if with_cache else 0)
        kbuf_ref[0:CHUNK, :] = jnp.zeros((CHUNK, kbuf_ref.shape[1]), BF16)
        vbuf_ref[0:CHUNK, :] = jnp.zeros((CHUNK, vbuf_ref.shape[1]), BF16)
        if with_cache:
            kbuf_ref[CHUNK:r, :] = ck_ref[0].astype(BF16)
            vbuf_ref[CHUNK:r, :] = cv_ref[0].astype(BF16)
        kbuf_ref[r:, :] = k_ref[0]
        vbuf_ref[r:, :] = v_ref[0]

    pairs = bias_ref.shape[0]
    pw = 2 * ATT_HEAD_DIM
    left = lax.broadcasted_iota(jnp.int32, (CHUNK, pw), 1) < ATT_HEAD_DIM
    units = [(ci, hp) for ci in range(cps) for hp in range(pairs)]

    def body(static_step):
        def geometry(u):
            if static_step is None:
                return BAND_CHUNKS + 1, False
            valid = min(static_step * cps + units[u][0] + first_chunk + 1, BAND_CHUNKS + 1)
            return valid + (1 - valid % 2), valid % 2 == 0

        def band(ref, u):
            ci, hp = units[u]
            take = geometry(u)[0] * CHUNK
            if static_step is None:
                first = (step * cps + ci + first_chunk - BAND_CHUNKS + 1) * CHUNK
                rows = pl.ds(pl.multiple_of(first, CHUNK), take)
            else:
                end = (static_step * cps + ci + first_chunk + 2) * CHUNK
                rows = slice(end - take, end)
            return ref[rows, hp * pw:(hp + 1) * pw]

        def scores(u):
            ci, hp = units[u]
            chunks, masked = geometry(u)
            qp = q_ref[0, ci * CHUNK:(ci + 1) * CHUNK, hp * pw:(hp + 1) * pw]
            zero = jnp.zeros_like(qp)
            q2 = jnp.concatenate([jnp.where(left, qp, zero), jnp.where(left, zero, qp)], axis=0)
            s = _dot_nt(q2, band(kbuf_ref, u)) + bias_ref[hp, :, BAND - chunks * CHUNK:]
            if masked:
                lane = lax.broadcasted_iota(jnp.int32, s.shape, 1)
                s = jnp.where(lane >= CHUNK, s, -jnp.inf)
            return s

        def softmax(s):
            pr = jnp.exp2(s - jnp.max(s, axis=-1, keepdims=True))
            return pr.astype(BF16), jnp.sum(pr, axis=-1, keepdims=True)

        def weighted(u, pr):
            v = band(vbuf_ref, u)
            past = (geometry(u)[0] - 1) * CHUNK
            own = _dot(pr[:, past:], v[past:])
            return own if past == 0 else _dot(pr[:, :past], v[:past]) + own

        def finish(u, r, denom):
            ci, hp = units[u]
            r = r / denom
            obuf_ref[ci * CHUNK:(ci + 1) * CHUNK, hp * pw:(hp + 1) * pw] = (
                jnp.where(left, r[:CHUNK], r[CHUNK:]).astype(obuf_ref.dtype))

        n = len(units)
        s, pr, den, acc = {}, {}, {}, {}
        for t in range(n + 3):
            if t < n:
                s[t] = scores(t)
            if 0 <= t - 2 < n:
                acc[t - 2] = weighted(t - 2, pr.pop(t - 2))
            if 0 <= t - 1 < n:
                pr[t - 1], den[t - 1] = softmax(s.pop(t - 1))
            if 0 <= t - 3 < n:
                finish(t - 3, acc.pop(t - 3), den.pop(t - 3))

    short_steps = max(0, BAND_CHUNKS - first_chunk) // cps
    for st in range(short_steps):
        pl.when(step == st)(functools.partial(body, st))
    pl.when(step >= short_steps)(functools.partial(body, None))

    for c0 in range(0, wo_ref.shape[1], COL_TILE):
        m_ref[0, :, c0:c0 + COL_TILE] = _dot(obuf_ref[...], wo_ref[:, c0:c0 + COL_TILE])


def _attention(q, k, v, bias, wo, *, first_chunk, cps, cache=None):
    b, l, d = q.shape
    lk = k.shape[1]
    assert BAND_CHUNKS % cps == 0 and l % (cps * CHUNK) == 0
    kern = functools.partial(_attn_kernel, first_chunk=first_chunk, cps=cps,
                             with_cache=cache is not None)
    cache = list(cache or ())
    seq = lambda rows: pl.BlockSpec((1, rows, d), lambda i, c: (i, 0, 0))
    band_rows = CHUNK + lk + sum(a.shape[1] for a in cache[:1])
    return pl.pallas_call(
        kern,
        grid=(b, l // (cps * CHUNK)),
        in_specs=[pl.BlockSpec((1, cps * CHUNK, d), lambda i, c: (i, c, 0)),
                  seq(lk), seq(lk), *[seq(a.shape[1]) for a in cache],
                  _resident(bias.shape), _resident(wo.shape)],
        out_specs=pl.BlockSpec((1, cps * CHUNK, wo.shape[1]), lambda i, c: (i, c, 0)),
        out_shape=jax.ShapeDtypeStruct((b, l, wo.shape[1]), F32),
        scratch_shapes=[pltpu.VMEM((band_rows, d), BF16), pltpu.VMEM((band_rows, d), BF16),
                        pltpu.VMEM((cps * CHUNK, d), BF16)],
        compiler_params=_cparams("parallel", "arbitrary"),
        name="band_attention",
    )(q, k, v, *cache, bias, wo)


def _row(v):
    return v.reshape(1, -1).astype(F32)


def _ssd_layer(x3, state, p, nw, *, q, nb):
    b, l, d = x3.shape
    rows = b * l
    d_inner = p["wout"].shape[0]
    conv_dim = p["convw"].shape[1]
    x = x3.reshape(rows, d)
    z, xbc, dt, dtT = _inproj(x, _row(nw[0]), p["win"], d_inner=d_inner, conv_dim=conv_dim,
                              nh=p["dtb"].shape[1])
    xbc3 = xbc.reshape(b, l, conv_dim)
    if state is not None:
        h0, conv_prev = state
        state = (h0, jnp.pad(conv_prev, ((0, 0), (CARRY_ROWS - (CONV_W - 1), 0), (0, 0))))
    dtT = jnp.transpose(dtT.reshape(-1, rows // q, q), (1, 0, 2))
    y, h_new = _ssd(xbc3, z.reshape(b, l, d_inner), dt, dtT, state,
                 p["convw"], p["convb"], p["dtb"], p["dtbT"], p["alog"], p["alogT"],
                 p["dskip"], p["gnw"], p["expand"], p["wout"], q=q, nb=nb)
    new_conv = xbc3[:, l - (CONV_W - 1):]
    return y.reshape(rows, d), h_new, new_conv


def _layer_tail(mixed, w_mix, x3, fp, nw, name):
    b, l, d = x3.shape
    out = _mix_ffn(mixed, w_mix, _row(nw[1]), x3.reshape(b * l, d), _row(nw[2]),
                   fp["win"], fp["wo"], _row(nw[3]), name)
    return out.reshape(b, l, d)


def kernel(x_prompt, x_sample, state_ssm, state_conv, cache_k, cache_v, norm_w,
           ssm_w_in, ssm_conv_w, ssm_conv_b, ssm_dt_bias, ssm_A_log, ssm_D, ssm_norm_w, ssm_w_out,
           kv_norm_w, w_kv, attn_w_q, attn_rel_bias, attn_w_o, ffn_w_in, ffn_w_out):
    bp, lp, d = x_prompt.shape
    bs, ls, _ = x_sample.shape
    n_a = ssm_w_in.shape[0]
    depth = norm_w.shape[0]
    d_inner = ssm_w_out.shape[1]
    heads = ssm_dt_bias.shape[1]
    conv_dim = ssm_conv_w.shape[2]
    att_dim = attn_w_q.shape[2]
    att_heads = att_dim // ATT_HEAD_DIM
    hidden = ffn_w_out.shape[1]

    xp, xs = x_prompt, x_sample
    ssm_p, conv_p, ssm_s, conv_s = [], [], [], []
    outs_kv = None
    for layer in range(depth):
        nw = norm_w[layer]
        fp = {"win": ffn_w_in[layer].astype(BF16), "wo": ffn_w_out[layer].astype(BF16)}
        if layer < n_a:
            a = layer
            sp = {"win": jnp.pad(ssm_w_in[a], ((0, 0), (0, 2 * SSM_HEAD_DIM - heads))).astype(BF16),
                  "convw": 0.5 * ssm_conv_w[a], "convb": _row(0.5 * ssm_conv_b[a]),
                  "dtb": _row(ssm_dt_bias[a]), "dtbT": ssm_dt_bias[a].reshape(-1, 1),
                  "alog": _row(ssm_A_log[a]), "alogT": ssm_A_log[a].reshape(-1, 1),
                  "dskip": _row(jnp.repeat(ssm_D[a], SSM_HEAD_DIM)),
                  "gnw": _row(ssm_norm_w[a]), "wout": ssm_w_out[a].astype(BF16),
                  "expand": jnp.repeat(jnp.eye(heads, dtype=BF16), SSM_HEAD_DIM, axis=1)}
            mp, hp_new, cp_new = _ssd_layer(xp, None, sp, nw, q=SSD_BLOCK, nb=SSD_BLOCKS_PER_STEP)
            ms, hs_new, cs_new = _ssd_layer(xs, (state_ssm[a], state_conv[a]), sp, nw, q=ls, nb=1)
            w_mix = None
            ssm_p.append(hp_new)
            conv_p.append(cp_new)
            ssm_s.append(hs_new)
            conv_s.append(cs_new)
        else:
            i = layer - n_a
            wq = attn_w_q[i].astype(BF16)
            wkv = w_kv.astype(BF16)
            wo = attn_w_o[i].astype(BF16)
            w_mix = None
            bias = _bias_table(attn_rel_bias[i]).reshape(att_heads // 2, 2 * CHUNK, BAND)
            rows_p = min(BAND_CHUNKS * CHUNK, lp)
            qp, kpb, vpb, kpf, vpf = _qkv(xp, _row(nw[0]), _row(kv_norm_w), wq, wkv,
                                          tm=ROW_TILE, tail_rows=rows_p)
            mp = _attention(qp, kpb, vpb, bias, wo, first_chunk=0,
                            cps=ATTN_CHUNKS_PER_STEP).reshape(bp * lp, d)
            flat = lambda a: a.reshape(bs, ls, att_dim)
            qs, ksb, vsb, ksf, vsf = map(flat, _qkv(xs.reshape(1, bs * ls, d), _row(nw[0]),
                                                    _row(kv_norm_w), wq, wkv, tm=ROW_TILE,
                                                    tail_rows=bs * ls))
            r = cache_k.shape[1]
            cache = (cache_k.astype(BF16).reshape(bs, r, att_dim),
                     cache_v.astype(BF16).reshape(bs, r, att_dim))
            ms = _attention(qs, ksb, vsb, bias, wo, first_chunk=BAND_CHUNKS, cps=1,
                            cache=cache).reshape(bs * ls, d)
            if outs_kv is None:
                outs_kv = (kpf.reshape(bp, rows_p, att_heads, ATT_HEAD_DIM),
                           vpf.reshape(bp, rows_p, att_heads, ATT_HEAD_DIM),
                           ksf.reshape(bs, ls, att_heads, ATT_HEAD_DIM),
                           vsf.reshape(bs, ls, att_heads, ATT_HEAD_DIM))
        xp = _layer_tail(mp, w_mix, xp, fp, nw, "mix_ffn_prompt")
        xs = _layer_tail(ms, w_mix, xs, fp, nw, "mix_ffn_sample")
    kp_out, vp_out, ks_out, vs_out = outs_kv
    return (xp, xs, jnp.stack(ssm_p), jnp.stack(conv_p), kp_out, vp_out,
            jnp.stack(ssm_s), jnp.stack(conv_s), ks_out, vs_out)
```
